```python
import jax, jax.numpy as jnp
from jax import lax
import numpy as np

D_MODEL = 1024
BATCH = 2
SEQ = 8192
DEPTH = 4
DEC_BATCH = 32
DEC_SEQ = 32
PAST_LEN = 2048

CHUNK = 64
A_CHUNK = 128
D_A = D_MODEL // 2
A_HEADS = 4
A_HEAD_DIM = D_A // A_HEADS
D_B = D_MODEL // 2
B_HEADS = 8
B_HEAD_DIM = D_B // B_HEADS
B_CONV = 4
LRU_C = 8.0
D_C = D_MODEL
C_CONV = 3
D_FF = 4 * D_MODEL
N_EVEN = (DEPTH + 1) // 2
N_ODD = DEPTH // 2
D_IN_EVEN = 2 * D_A + 2 * D_B
EPS = 1e-6

kernel_name = "hybrid_gmlp_rglru_shortconv_stream_step"


def rmsnorm(x, g):
    xf = x.astype(jnp.float32)
    y = xf * lax.rsqrt(jnp.mean(xf * xf, axis=-1, keepdims=True) + EPS)
    return (y * g.astype(jnp.float32)).astype(x.dtype)


def causal_dwconv(x, buf, w):
    width = w.shape[0]
    T = x.shape[1]
    xf = jnp.concatenate([buf.astype(x.dtype), x], axis=1)
    y = w[0] * xf[:, 0:T]
    for k in range(1, width):
        y = y + w[k] * xf[:, k:k + T]
    return y, xf[:, -(width - 1):]


def spatial_gating(u, v, w_s, b_s):
    bsz, T, _ = v.shape
    L = min(T, A_CHUNK)
    n = T // L
    pos = jnp.arange(L)
    mask = (pos[None, :] // CHUNK) <= (pos[:, None] // CHUNK)
    w = jnp.where(mask[None], w_s[:, :L, :L], 0).astype(v.dtype)
    vc = v.reshape(bsz, n, L, A_HEADS, A_HEAD_DIM)
    s = jnp.einsum('hij,bnjhc->bnihc', w, vc) + b_s[:, :L].T[None, None, :, :, None].astype(v.dtype)
    return u * s.reshape(bsz, T, D_A)


def rg_lru(x, h0, wa, ba, wx, bx, lam):
    bsz, T, _ = x.shape
    xh = x.reshape(bsz, T, B_HEADS, B_HEAD_DIM)
    r = jax.nn.sigmoid((jnp.einsum('bthi,hij->bthj', xh, wa).reshape(bsz, T, D_B) + ba).astype(jnp.float32))
    ig = jax.nn.sigmoid((jnp.einsum('bthi,hij->bthj', xh, wx).reshape(bsz, T, D_B) + bx).astype(jnp.float32))
    log_a = -LRU_C * r * jax.nn.softplus(-lam.astype(jnp.float32))
    a = jnp.exp(log_a)
    bterm = jnp.sqrt(-jnp.expm1(2.0 * log_a)) * ig * x.astype(jnp.float32)

    def combine(p, q):
        a1, b1 = p
        a2, b2 = q
        return a1 * a2, a2 * b1 + b2

    a_cum, b_cum = lax.associative_scan(combine, (a, bterm), axis=1)
    h = b_cum + a_cum * h0.astype(jnp.float32)[:, None]
    return h.astype(x.dtype), h[:, -1].astype(x.dtype)


def mixer_even(h, conv_buf, h0, w_in, v_gain, w_s, b_s, conv_w, conv_b, wa, ba, wx, bx, lam, w_out):
    z = h @ w_in
    u, v, xb, gb = jnp.split(z, [D_A, 2 * D_A, 2 * D_A + D_B], axis=-1)
    u = jax.nn.gelu(u)
    v = rmsnorm(jax.nn.gelu(v), v_gain)
    a_out = spatial_gating(u, v, w_s, b_s)
    xc, new_buf = causal_dwconv(xb, conv_buf, conv_w)
    hr, h_last = rg_lru(xc + conv_b, h0, wa, ba, wx, bx, lam)
    b_out = hr * jax.nn.gelu(gb)
    out = jnp.concatenate([a_out, b_out], axis=-1) @ w_out
    return out, v, new_buf, h_last


def mixer_odd(h, conv_buf, w_in, conv_w, w_out):
    bg, cg, xv = jnp.split(h @ w_in, 3, axis=-1)
    y, new_buf = causal_dwconv(cg * xv, conv_buf, conv_w)
    return (bg * y) @ w_out, new_buf


def trunk(x, b_conv, b_h, c_conv, p):
    v_rows, b_conv_new, b_h_new, c_conv_new = [], [], [], []
    for l in range(DEPTH):
        hn = rmsnorm(x, p['norm_mix_pre'][l])
        if l % 2 == 0:
            e = l // 2
            m, v, bc, bh = mixer_even(hn, b_conv[e], b_h[e], p['w_in_even'][e], p['a_v_gain'][e],
                                      p['a_w_s'][e], p['a_b_s'][e], p['b_conv_w'][e], p['b_conv_b'][e],
                                      p['b_wa'][e], p['b_ba'][e], p['b_wx'][e], p['b_bx'][e],
                                      p['b_lambda'][e], p['w_out_even'][e])
            v_rows.append(v)
            b_conv_new.append(bc)
            b_h_new.append(bh)
        else:
            o = l // 2
            m, cc = mixer_odd(hn, c_conv[o], p['c_w_in'][o], p['c_conv_w'][o], p['c_w_out'][o])
            c_conv_new.append(cc)
        x = x + rmsnorm(m, p['norm_mix_post'][l])
        hn = rmsnorm(x, p['norm_ffn_pre'][l])
        f = jnp.square(jax.nn.relu(hn @ p['mlp_up'][l])) @ p['mlp_down'][l]
        x = x + rmsnorm(f, p['norm_ffn_post'][l])
    return x, jnp.stack(v_rows), jnp.stack(b_conv_new), jnp.stack(b_h_new), jnp.stack(c_conv_new)


def setup_inputs(seed: int = 0) -> dict:
    key = jax.random.key(seed)
    ks = jax.random.split(key, 32)
    nrm = lambda k, shape, s: jax.random.normal(k, shape, jnp.float32) * s
    gain = lambda k, shape: 1.0 + 0.05 * jax.random.normal(k, shape, jnp.float32)
    a_init = jax.random.uniform(ks[19], (N_EVEN, D_B), jnp.float32, 0.9, 0.999)
    return {
        'x_prompt': nrm(ks[0], (BATCH, SEQ, D_MODEL), 1.0),
        'x_sample': nrm(ks[1], (DEC_BATCH, DEC_SEQ, D_MODEL), 1.0),
        'cache_b_conv': nrm(ks[2], (N_EVEN, DEC_BATCH, B_CONV - 1, D_B), 1.0),
        'state_b_h': nrm(ks[3], (N_EVEN, DEC_BATCH, D_B), 0.5),
        'cache_c_conv': nrm(ks[4], (N_ODD, DEC_BATCH, C_CONV - 1, D_C), 1.0),
        'norm_mix_pre': gain(ks[5], (DEPTH, D_MODEL)),
        'norm_mix_post': gain(ks[6], (DEPTH, D_MODEL)),
        'norm_ffn_pre': gain(ks[7], (DEPTH, D_MODEL)),
        'norm_ffn_post': gain(ks[8], (DEPTH, D_MODEL)),
        'w_in_even': nrm(ks[9], (N_EVEN, D_MODEL, D_IN_EVEN), D_MODEL ** -0.5),
        'a_v_gain': gain(ks[10], (N_EVEN, D_A)),
        'a_w_s': nrm(ks[11], (N_EVEN, A_HEADS, A_CHUNK, A_CHUNK), A_CHUNK ** -0.5),
        'a_b_s': 1.0 + 0.1 * jax.random.normal(ks[12], (N_EVEN, A_HEADS, A_CHUNK), jnp.float32),
        'b_conv_w': nrm(ks[13], (N_EVEN, B_CONV, D_B), B_CONV ** -0.5),
        'b_conv_b': nrm(ks[14], (N_EVEN, D_B), 0.01),
        'b_wa': nrm(ks[15], (N_EVEN, B_HEADS, B_HEAD_DIM, B_HEAD_DIM), B_HEAD_DIM ** -0.5),
        'b_ba': nrm(ks[16], (N_EVEN, D_B), 0.1),
        'b_wx': nrm(ks[17], (N_EVEN, B_HEADS, B_HEAD_DIM, B_HEAD_DIM), B_HEAD_DIM ** -0.5),
        'b_bx': nrm(ks[18], (N_EVEN, D_B), 0.1),
        'b_lambda': jnp.log(a_init) - jnp.log1p(-a_init),
        'w_out_even': nrm(ks[20], (N_EVEN, D_A + D_B, D_MODEL), (D_A + D_B) ** -0.5),
        'c_w_in': nrm(ks[21], (N_ODD, D_MODEL, 3 * D_C), D_MODEL ** -0.5),
        'c_conv_w': nrm(ks[22], (N_ODD, C_CONV, D_C), C_CONV ** -0.5),
        'c_w_out': nrm(ks[23], (N_ODD, D_C, D_MODEL), D_C ** -0.5),
        'mlp_up': nrm(ks[24], (DEPTH, D_MODEL, D_FF), D_MODEL ** -0.5),
        'mlp_down': nrm(ks[25], (DEPTH, D_FF, D_MODEL), D_FF ** -0.5),
    }


def reference(x_prompt, x_sample, cache_b_conv, state_b_h, cache_c_conv,
              norm_mix_pre, norm_mix_post, norm_ffn_pre, norm_ffn_post,
              w_in_even, a_v_gain, a_w_s, a_b_s, b_conv_w, b_conv_b, b_wa, b_ba, b_wx, b_bx,
              b_lambda, w_out_even, c_w_in, c_conv_w, c_w_out, mlp_up, mlp_down):
    p = dict(norm_mix_pre=norm_mix_pre, norm_mix_post=norm_mix_post,
             norm_ffn_pre=norm_ffn_pre, norm_ffn_post=norm_ffn_post,
             w_in_even=w_in_even, a_v_gain=a_v_gain, a_w_s=a_w_s, a_b_s=a_b_s,
             b_conv_w=b_conv_w, b_conv_b=b_conv_b, b_wa=b_wa, b_ba=b_ba, b_wx=b_wx, b_bx=b_bx,
             b_lambda=b_lambda, w_out_even=w_out_even, c_w_in=c_w_in, c_conv_w=c_conv_w,
             c_w_out=c_w_out, mlp_up=mlp_up, mlp_down=mlp_down)
    dt = x_prompt.dtype
    zb = jnp.zeros((N_EVEN, BATCH, B_CONV - 1, D_B), dt)
    zh = jnp.zeros((N_EVEN, BATCH, D_B), dt)
    zc = jnp.zeros((N_ODD, BATCH, C_CONV - 1, D_C), dt)
    y_prompt, _, bconv_p, bh_p, cconv_p = trunk(x_prompt, zb, zh, zc, p)
    y_sample, v_s, bconv_s, bh_s, cconv_s = trunk(x_sample, cache_b_conv, state_b_h, cache_c_conv, p)
    return (y_prompt, y_sample, v_s, bconv_p, bh_p, cconv_p, bconv_s, bh_s, cconv_s)
```

```python
import functools
from typing import NamedTuple

import jax
import jax.numpy as jnp
from jax import lax
from jax.experimental import pallas as pl
from jax.experimental.pallas import tpu as pltpu

D_MODEL = 1024
DEPTH = 4
CHUNK = 64
A_CHUNK = 128
D_A = D_MODEL // 2
A_HEADS = 4
A_HEAD_DIM = D_A // A_HEADS
D_B = D_MODEL // 2
B_HEADS = 8
B_HEAD_DIM = D_B // B_HEADS
B_CONV = 4
LRU_C = 8.0
D_C = D_MODEL
C_CONV = 3
D_FF = 4 * D_MODEL
EPS = 1e-6

SUBLANES = 8
V7X_VMEM_BYTES = 64 * 1024 * 1024
VMEM_LIMIT_BYTES = V7X_VMEM_BYTES - 8 * 1024 * 1024
ROW_TILE = 256

F32 = jnp.float32
BF16 = jnp.bfloat16


class TileCfg(NamedTuple):
    n_seq: int
    seq_rows: int
    tiles_per_seq: int
    gate_chunk: int
    emit_v: bool


def _rmsnorm(x, g):
    ms = jnp.mean(x * x, axis=-1, keepdims=True)
    return x * lax.rsqrt(ms + EPS) * g


def _shift_rows(x, hist, k, cfg):
    if k == 0:
        return x
    rolled = pltpu.roll(x, k, 0)
    row = lax.broadcasted_iota(jnp.int32, (SUBLANES, x.shape[1]), 0)
    pieces = []
    for s in range(cfg.n_seq):
        lo = s * cfg.seq_rows
        h = pltpu.roll(hist[s * SUBLANES:(s + 1) * SUBLANES], k, 0)
        pieces.append(jnp.where(row < k, h, rolled[lo:lo + SUBLANES]))
        pieces.append(rolled[lo + SUBLANES:lo + cfg.seq_rows])
    return jnp.concatenate(pieces, axis=0)


def _causal_conv(x, hist, w, cfg):
    width = w.shape[0]
    y = w[0:1] * _shift_rows(x, hist, width - 1, cfg)
    for k in range(1, width):
        y = y + w[k:k + 1] * _shift_rows(x, hist, width - 1 - k, cfg)
    return y


def _seq_tails(x, cfg):
    pieces = [x[(s + 1) * cfg.seq_rows - SUBLANES:(s + 1) * cfg.seq_rows] for s in range(cfg.n_seq)]
    return pieces[0] if len(pieces) == 1 else jnp.concatenate(pieces, axis=0)


def _linear_scan(a, b, h_prev, cfg):
    rows, _ = a.shape
    row = lax.broadcasted_iota(jnp.int32, a.shape, 0) & (SUBLANES - 1)
    d = 1
    while d < SUBLANES:
        keep = row >= d
        a_sh = pltpu.roll(a, d, 0)
        b_sh = pltpu.roll(b, d, 0)
        b = jnp.where(keep, a * b_sh + b, b)
        a = jnp.where(keep, a * a_sh, a)
        d *= 2
    groups_per_seq = cfg.seq_rows // SUBLANES
    out = []
    for s in range(cfg.n_seq):
        carry = h_prev[s * SUBLANES + SUBLANES - 1:(s + 1) * SUBLANES]
        for g in range(groups_per_seq):
            lo = s * cfg.seq_rows + g * SUBLANES
            hg = b[lo:lo + SUBLANES] + a[lo:lo + SUBLANES] * carry
            out.append(hg)
            carry = hg[SUBLANES - 1:SUBLANES]
    del rows
    return jnp.concatenate(out, axis=0)


def _spatial_gating(v, ws_ref, bias_ref, cfg):
    lc = cfg.gate_chunk
    n_chunks = v.shape[0] // lc
    vb = v.astype(BF16)
    pi = lax.shift_right_logical(lax.broadcasted_iota(jnp.int32, (lc, lc), 0), 6)
    pj = lax.shift_right_logical(lax.broadcasted_iota(jnp.int32, (lc, lc), 1), 6)
    mask = pj <= pi
    heads = []
    for h in range(A_HEADS):
        lanes = slice(h * A_HEAD_DIM, (h + 1) * A_HEAD_DIM)
        w = jnp.where(mask, ws_ref[h, 0:lc, 0:lc], 0.0).astype(BF16)
        vh = jnp.concatenate([vb[c * lc:(c + 1) * lc, lanes] for c in range(n_chunks)], axis=1)
        sh = jnp.dot(w, vh, preferred_element_type=F32)
        bias = bias_ref[h, 0:lc, :]
        heads.append(jnp.concatenate(
            [sh[:, c * A_HEAD_DIM:(c + 1) * A_HEAD_DIM] + bias for c in range(n_chunks)], axis=0))
    return jnp.concatenate(heads, axis=1)


def _mlp_residual(x, gpre_ref, up_ref, down_ref, gpost_ref):
    hn = _rmsnorm(x, gpre_ref[...]).astype(BF16)
    hid = jnp.dot(hn, up_ref[...], preferred_element_type=F32)
    act = jnp.square(jnp.maximum(hid, 0.0)).astype(BF16)
    f = jnp.dot(act, down_ref[...], preferred_element_type=F32)
    return x + _rmsnorm(f, gpost_ref[...])


def _even_layer_kernel(cfg, x_ref, gpre_ref, win_ref, vgain_ref, ws_ref, bias_ref, convw_ref, convb_ref,
                       wgate_ref, ba_ref, bx_ref, lam_ref, wout_ref, gpost_ref, gfpre_ref, up_ref,
                       down_ref, gfpost_ref, hist_in_ref, h_in_ref, *rest):
    if cfg.emit_v:
        xo_ref, tailx_ref, tailh_ref, v_ref, hist_scr, h_scr = rest
    else:
        xo_ref, tailx_ref, tailh_ref, hist_scr, h_scr = rest
        v_ref = None

    @pl.when(pl.program_id(0) % cfg.tiles_per_seq == 0)
    def _():
        hist_scr[...] = hist_in_ref[...]
        h_scr[...] = h_in_ref[...]

    x = x_ref[...]
    hn = _rmsnorm(x, gpre_ref[...]).astype(BF16)
    z = jnp.dot(hn, win_ref[...], preferred_element_type=F32)
    u = jax.nn.gelu(z[:, 0:D_A])
    v = _rmsnorm(jax.nn.gelu(z[:, D_A:2 * D_A]), vgain_ref[...])
    xb = z[:, 2 * D_A:2 * D_A + D_B]
    gb = z[:, 2 * D_A + D_B:]
    if v_ref is not None:
        v_ref[...] = v

    a_out = u * _spatial_gating(v, ws_ref, bias_ref, cfg)

    xc = _causal_conv(xb, hist_scr[...], convw_ref[...], cfg) + convb_ref[...]
    new_hist = _seq_tails(xb, cfg)
    hist_scr[...] = new_hist
    tailx_ref[...] = new_hist

    half = D_B // 2
    xcb = xc.astype(BF16)
    g0 = jnp.dot(xcb[:, :half], wgate_ref[0], preferred_element_type=F32)
    g1 = jnp.dot(xcb[:, half:], wgate_ref[1], preferred_element_type=F32)
    r = jax.nn.sigmoid(jnp.concatenate([g0[:, :half], g1[:, :half]], axis=1) + ba_ref[...])
    ig = jax.nn.sigmoid(jnp.concatenate([g0[:, half:], g1[:, half:]], axis=1) + bx_ref[...])
    nl = -lam_ref[...]
    softplus = jnp.maximum(nl, 0.0) + jnp.log1p(jnp.exp(-jnp.abs(nl)))
    log_a = -LRU_C * r * softplus
    a = jnp.exp(log_a)
    bterm = jnp.sqrt(1.0 - a * a) * ig * xc
    hseq = _linear_scan(a, bterm, h_scr[...], cfg)
    new_h = _seq_tails(hseq, cfg)
    h_scr[...] = new_h
    tailh_ref[...] = new_h

    b_out = hseq * jax.nn.gelu(gb)
    mix = jnp.concatenate([a_out, b_out], axis=1).astype(BF16)
    m = jnp.dot(mix, wout_ref[...], preferred_element_type=F32)
    x1 = x + _rmsnorm(m, gpost_ref[...])
    xo_ref[...] = _mlp_residual(x1, gfpre_ref, up_ref, down_ref, gfpost_ref)


def _odd_layer_kernel(cfg, x_ref, gpre_ref, win_ref, convw_ref, wout_ref, gpost_ref, gfpre_ref, up_ref,
                      down_ref, gfpost_ref, hist_in_ref, xo_ref, tail_ref, hist_scr):
    @pl.when(pl.program_id(0) % cfg.tiles_per_seq == 0)
    def _():
        hist_scr[...] = hist_in_ref[...]

    x = x_ref[...]
    hn = _rmsnorm(x, gpre_ref[...]).astype(BF16)
    z = jnp.dot(hn, win_ref[...], preferred_element_type=F32)
    bg = z[:, 0:D_C]
    p = z[:, D_C:2 * D_C] * z[:, 2 * D_C:]
    y = _causal_conv(p, hist_scr[...], convw_ref[...], cfg)
    new_hist = _seq_tails(p, cfg)
    hist_scr[...] = new_hist
    tail_ref[...] = new_hist
    m = jnp.dot((bg * y).astype(BF16), wout_ref[...], preferred_element_type=F32)
    x1 = x + _rmsnorm(m, gpost_ref[...])
    xo_ref[...] = _mlp_residual(x1, gfpre_ref, up_ref, down_ref, gfpost_ref)


def _resident(arr):
    nd = arr.ndim
    return pl.BlockSpec(arr.shape, lambda i, _nd=nd: (0,) * _nd, pipeline_mode=pl.Buffered(1))


def _state_spec(cfg, width):
    return pl.BlockSpec((cfg.n_seq * SUBLANES, width), lambda i: (i // cfg.tiles_per_seq, 0))


def _row_spec(width):
    return pl.BlockSpec((ROW_TILE, width), lambda i: (i, 0))


def _compiler_params():
    return pltpu.CompilerParams(dimension_semantics=("arbitrary",), vmem_limit_bytes=VMEM_LIMIT_BYTES)


def _even_layer(cfg, x, hist, h0, w):
    rows = x.shape[0]
    n_state_rows = hist.shape[0]
    consts = [w["gpre"], w["win"], w["vgain"], w["ws"], w["bias"], w["convw"], w["convb"], w["wgate"],
              w["ba"], w["bx"], w["lam"], w["wout"], w["gpost"], w["gfpre"], w["up"], w["down"], w["gfpost"]]
    in_specs = ([_row_spec(D_MODEL)] + [_resident(c) for c in consts]
                + [_state_spec(cfg, D_B), _state_spec(cfg, D_B)])
    out_shape = [jax.ShapeDtypeStruct((rows, D_MODEL), F32),
                 jax.ShapeDtypeStruct((n_state_rows, D_B), F32),
                 jax.ShapeDtypeStruct((n_state_rows, D_B), F32)]
    out_specs = [_row_spec(D_MODEL), _state_spec(cfg, D_B), _state_spec(cfg, D_B)]
    if cfg.emit_v:
        out_shape.append(jax.ShapeDtypeStruct((rows, D_A), F32))
        out_specs.append(_row_spec(D_A))
    scratch = [pltpu.VMEM((cfg.n_seq * SUBLANES, D_B), F32), pltpu.VMEM((cfg.n_seq * SUBLANES, D_B), F32)]
    return pl.pallas_call(
        functools.partial(_even_layer_kernel, cfg),
        grid=(rows // ROW_TILE,),
        in_specs=in_specs,
        out_specs=out_specs,
        out_shape=out_shape,
        scratch_shapes=scratch,
        compiler_params=_compiler_params(),
        name="even_layer",
    )(x, *consts, hist, h0)


def _odd_layer(cfg, x, hist, w):
    rows = x.shape[0]
    consts = [w["gpre"], w["win"], w["convw"], w["wout"], w["gpost"], w["gfpre"], w["up"], w["down"],
              w["gfpost"]]
    in_specs = [_row_spec(D_MODEL)] + [_resident(c) for c in consts] + [_state_spec(cfg, D_C)]
    out_shape = [jax.ShapeDtypeStruct((rows, D_MODEL), F32),
                 jax.ShapeDtypeStruct((hist.shape[0], D_C), F32)]
    out_specs = [_row_spec(D_MODEL), _state_spec(cfg, D_C)]
    return pl.pallas_call(
        functools.partial(_odd_layer_kernel, cfg),
        grid=(rows // ROW_TILE,),
        in_specs=in_specs,
        out_specs=out_specs,
        out_shape=out_shape,
        scratch_shapes=[pltpu.VMEM((cfg.n_seq * SUBLANES, D_C), F32)],
        compiler_params=_compiler_params(),
        name="odd_layer",
    )(x, *consts, hist)


def _block_diag_gates(wa, wx):
    heads_per_half = B_HEADS // 2
    eye = jnp.eye(heads_per_half, dtype=wa.dtype)

    def bd(w4):
        return jnp.einsum("hij,hg->higj", w4, eye).reshape(D_B // 2, D_B // 2)

    halves = [jnp.concatenate([bd(wa[q * heads_per_half:(q + 1) * heads_per_half]),
                               bd(wx[q * heads_per_half:(q + 1) * heads_per_half])], axis=1)
              for q in range(2)]
    return jnp.stack(halves).astype(BF16)


def _pad_state(rows, width_rows):
    n, r, c = rows.shape
    del width_rows
    return jnp.pad(rows, ((0, 0), (SUBLANES - r, 0), (0, 0))).reshape(n * SUBLANES, c)


def _trunk(x, b_conv, b_h, c_conv, p, cfg):
    tails_b, tails_h, tails_c, v_rows = [], [], [], []
    for l in range(DEPTH):
        common = dict(gpre=p["norm_mix_pre"][l][None], gpost=p["norm_mix_post"][l][None],
                      gfpre=p["norm_ffn_pre"][l][None], gfpost=p["norm_ffn_post"][l][None],
                      up=p["mlp_up"][l], down=p["mlp_down"][l])
        if l % 2 == 0:
            e = l // 2
            w = dict(common, win=p["w_in_even"][e], vgain=p["a_v_gain"][e][None], ws=p["a_w_s"][e],
                     bias=p["a_bias"][e], convw=p["b_conv_w"][e], convb=p["b_conv_b"][e][None],
                     wgate=p["b_wgate"][e], ba=p["b_ba"][e][None], bx=p["b_bx"][e][None],
                     lam=p["b_lambda"][e][None], wout=p["w_out_even"][e])
            outs = _even_layer(cfg, x, b_conv[e], b_h[e], w)
            x = outs[0]
            tails_b.append(outs[1])
            tails_h.append(outs[2])
            if cfg.emit_v:
                v_rows.append(outs[3])
        else:
            o = l // 2
            w = dict(common, win=p["c_w_in"][o], convw=p["c_conv_w"][o], wout=p["c_w_out"][o])
            x, tail = _odd_layer(cfg, x, c_conv[o], w)
            tails_c.append(tail)
    return x, tails_b, tails_h, tails_c, v_rows


def kernel(x_prompt, x_sample, cache_b_conv, state_b_h, cache_c_conv, norm_mix_pre, norm_mix_post, norm_ffn_pre, norm_ffn_post, w_in_even, a_v_gain, a_w_s, a_b_s, b_conv_w, b_conv_b, b_wa, b_ba, b_wx, b_bx, b_lambda, w_out_even, c_w_in, c_conv_w, c_w_out, mlp_up, mlp_down):
    batch, seq, _ = x_prompt.shape
    dec_batch, dec_seq, _ = x_sample.shape
    n_even, n_odd = w_in_even.shape[0], c_w_in.shape[0]
    p = dict(norm_mix_pre=norm_mix_pre, norm_mix_post=norm_mix_post, norm_ffn_pre=norm_ffn_pre,
             norm_ffn_post=norm_ffn_post, w_in_even=w_in_even.astype(BF16), a_v_gain=a_v_gain, a_w_s=a_w_s,
             a_bias=jnp.broadcast_to(a_b_s[..., None], a_b_s.shape + (A_HEAD_DIM,)),
             b_conv_w=b_conv_w, b_conv_b=b_conv_b,
             b_wgate=jnp.stack([_block_diag_gates(b_wa[e], b_wx[e]) for e in range(n_even)]),
             b_ba=b_ba, b_bx=b_bx, b_lambda=b_lambda, w_out_even=w_out_even.astype(BF16),
             c_w_in=c_w_in.astype(BF16), c_conv_w=c_conv_w, c_w_out=c_w_out.astype(BF16),
             mlp_up=mlp_up.astype(BF16), mlp_down=mlp_down.astype(BF16))

    cfg_p = TileCfg(n_seq=1, seq_rows=ROW_TILE, tiles_per_seq=seq // ROW_TILE, gate_chunk=A_CHUNK,
                    emit_v=False)
    zb = jnp.zeros((n_even, batch * SUBLANES, D_B), F32)
    zc = jnp.zeros((n_odd, batch * SUBLANES, D_C), F32)
    y_p, tb_p, th_p, tc_p, _ = _trunk(x_prompt.reshape(batch * seq, D_MODEL), zb, zb, zc, p, cfg_p)

    cfg_s = TileCfg(n_seq=ROW_TILE // dec_seq, seq_rows=dec_seq, tiles_per_seq=1, gate_chunk=dec_seq,
                    emit_v=True)
    sb = jnp.stack([_pad_state(cache_b_conv[e], SUBLANES) for e in range(n_even)])
    sh = jnp.stack([_pad_state(state_b_h[e][:, None, :], SUBLANES) for e in range(n_even)])
    sc = jnp.stack([_pad_state(cache_c_conv[o], SUBLANES) for o in range(n_odd)])
    y_s, tb_s, th_s, tc_s, v_s = _trunk(x_sample.reshape(dec_batch * dec_seq, D_MODEL), sb, sh, sc, p, cfg_s)

    def tails(ts, n, keep):
        t = jnp.stack(ts).reshape(len(ts), n, SUBLANES, -1)
        return t[:, :, SUBLANES - keep:, :]

    return (y_p.reshape(batch, seq, D_MODEL),
            y_s.reshape(dec_batch, dec_seq, D_MODEL),
            jnp.stack(v_s).reshape(n_even, dec_batch, dec_seq, D_A),
            tails(tb_p, batch, B_CONV - 1),
            tails(th_p, batch, 1)[:, :, 0, :],
            tails(tc_p, batch, C_CONV - 1),
            tails(tb_s, dec_batch, B_CONV - 1),
            tails(th_s, dec_batch, 1)[:, :, 0, :],
            tails(tc_s, dec_batch, C_CONV - 1))
```

```python
import functools
from typing import NamedTuple

import jax
import jax.numpy as jnp
from jax import lax
from jax.experimental import pallas as pl
from jax.experimental.pallas import tpu as pltpu

D_MODEL = 1024
DEPTH = 4
CHUNK = 64
A_CHUNK = 128
D_A = D_MODEL // 2
A_HEADS = 4
A_HEAD_DIM = D_A // A_HEADS
D_B = D_MODEL // 2
B_HEADS = 8
B_HEAD_DIM = D_B // B_HEADS
B_CONV = 4
LRU_C = 8.0
D_C = D_MODEL
C_CONV = 3
D_FF = 4 * D_MODEL
EPS = 1e-6

SUBLANES = 8
V7X_VMEM_BYTES = 64 * 1024 * 1024
VMEM_LIMIT_BYTES = V7X_VMEM_BYTES - 8 * 1024 * 1024
ROW_TILE = 256

F32 = jnp.float32
BF16 = jnp.bfloat16


class TileCfg(NamedTuple):
    n_seq: int
    seq_rows: int
    tiles_per_seq: int
    gate_chunk: int
    emit_v: bool
    n_tiles: int
    skewed: bool


def _rmsnorm(x, g):
    ms = jnp.mean(x * x, axis=-1, keepdims=True)
    return x * lax.rsqrt(ms + EPS) * g


def _shift_rows(x, hist, k, cfg):
    if k == 0:
        return x
    rolled = pltpu.roll(x, k, 0)
    row = lax.broadcasted_iota(jnp.int32, (SUBLANES, x.shape[1]), 0)
    pieces = []
    for s in range(cfg.n_seq):
        lo = s * cfg.seq_rows
        h = pltpu.roll(hist[s * SUBLANES:(s + 1) * SUBLANES], k, 0)
        pieces.append(jnp.where(row < k, h, rolled[lo:lo + SUBLANES]))
        pieces.append(rolled[lo + SUBLANES:lo + cfg.seq_rows])
    return jnp.concatenate(pieces, axis=0)


def _causal_conv(x, hist, w, cfg):
    width = w.shape[0]
    y = w[0:1] * _shift_rows(x, hist, width - 1, cfg)
    for k in range(1, width):
        y = y + w[k:k + 1] * _shift_rows(x, hist, width - 1 - k, cfg)
    return y


def _seq_tails(x, cfg):
    pieces = [x[(s + 1) * cfg.seq_rows - SUBLANES:(s + 1) * cfg.seq_rows] for s in range(cfg.n_seq)]
    return pieces[0] if len(pieces) == 1 else jnp.concatenate(pieces, axis=0)


def _linear_scan(a, b, h_prev, cfg):
    row = lax.broadcasted_iota(jnp.int32, a.shape, 0) & (SUBLANES - 1)
    d = 1
    while d < SUBLANES:
        keep = row >= d
        a_sh = pltpu.roll(a, d, 0)
        b_sh = pltpu.roll(b, d, 0)
        b = jnp.where(keep, a * b_sh + b, b)
        a = jnp.where(keep, a * a_sh, a)
        d *= 2
    groups_per_seq = cfg.seq_rows // SUBLANES
    out = []
    for s in range(cfg.n_seq):
        carry = h_prev[s * SUBLANES + SUBLANES - 1:(s + 1) * SUBLANES]
        for g in range(groups_per_seq):
            lo = s * cfg.seq_rows + g * SUBLANES
            hg = b[lo:lo + SUBLANES] + a[lo:lo + SUBLANES] * carry
            out.append(hg)
            carry = hg[SUBLANES - 1:SUBLANES]
    return jnp.concatenate(out, axis=0)


def _spatial_gating(v, ws_ref, bias_ref, cfg):
    lc = cfg.gate_chunk
    n_chunks = v.shape[0] // lc
    vb = v.astype(BF16)
    pi = lax.shift_right_logical(lax.broadcasted_iota(jnp.int32, (lc, lc), 0), 6)
    pj = lax.shift_right_logical(lax.broadcasted_iota(jnp.int32, (lc, lc), 1), 6)
    mask = pj <= pi
    heads = []
    for h in range(A_HEADS):
        lanes = slice(h * A_HEAD_DIM, (h + 1) * A_HEAD_DIM)
        w = jnp.where(mask, ws_ref[h, 0:lc, 0:lc], 0.0).astype(BF16)
        vh = jnp.concatenate([vb[c * lc:(c + 1) * lc, lanes] for c in range(n_chunks)], axis=1)
        sh = jnp.dot(w, vh, preferred_element_type=F32)
        bias = bias_ref[h, 0:lc, :]
        heads.append(jnp.concatenate(
            [sh[:, c * A_HEAD_DIM:(c + 1) * A_HEAD_DIM] + bias for c in range(n_chunks)], axis=0))
    return jnp.concatenate(heads, axis=1)


def _mlp_up(x, gpre_ref, up_ref):
    hn = _rmsnorm(x, gpre_ref[...]).astype(BF16)
    return jnp.dot(hn, up_ref[...], preferred_element_type=F32)


def _mlp_down(x, hid, down_ref, gpost_ref):
    act = jnp.square(jnp.maximum(hid, 0.0)).astype(BF16)
    f = jnp.dot(act, down_ref[...], preferred_element_type=F32)
    return x + _rmsnorm(f, gpost_ref[...])


def _tile_is_real(cfg):
    return pl.program_id(0) < cfg.n_tiles


def _load_states_at_sequence_start(cfg, pairs, x1_scr):
    i = pl.program_id(0)

    @pl.when(jnp.logical_and(lax.rem(i, cfg.tiles_per_seq) == 0, _tile_is_real(cfg)))
    def _():
        for scr, src in pairs:
            scr[...] = src[...]

    if cfg.skewed:
        @pl.when(i == 0)
        def _():
            x1_scr[...] = jnp.zeros_like(x1_scr)


def _keep_state(cfg, new, old):
    return jnp.where(_tile_is_real(cfg), new, old) if cfg.skewed else new


def _even_layer_kernel(cfg, x_ref, gpre_ref, win_ref, vgain_ref, ws_ref, bias_ref, convw_ref, convb_ref,
                       wgate_ref, ba_ref, bx_ref, lam_ref, wout_ref, gpost_ref, gfpre_ref, up_ref,
                       down_ref, gfpost_ref, hist_in_ref, h_in_ref, *rest):
    rest = list(rest)
    xo_ref, tailx_ref, tailh_ref = rest[:3]
    v_ref = rest[3] if cfg.emit_v else None
    hist_scr, h_scr = rest[-3:-1] if cfg.skewed else rest[-2:]
    x1_scr = rest[-1] if cfg.skewed else None

    _load_states_at_sequence_start(cfg, [(hist_scr, hist_in_ref), (h_scr, h_in_ref)], x1_scr)

    x = x_ref[...]
    hn = _rmsnorm(x, gpre_ref[...]).astype(BF16)
    z = jnp.dot(hn, win_ref[...], preferred_element_type=F32)
    if cfg.skewed:
        x1_prev = x1_scr[...]
        hid_prev = _mlp_up(x1_prev, gfpre_ref, up_ref)

    u = jax.nn.gelu(z[:, 0:D_A])
    v = _rmsnorm(jax.nn.gelu(z[:, D_A:2 * D_A]), vgain_ref[...])
    xb = z[:, 2 * D_A:2 * D_A + D_B]
    gb = z[:, 2 * D_A + D_B:]
    if v_ref is not None:
        v_ref[...] = v
    hist = hist_scr[...]
    xc = _causal_conv(xb, hist, convw_ref[...], cfg) + convb_ref[...]
    new_hist = _keep_state(cfg, _seq_tails(xb, cfg), hist)
    hist_scr[...] = new_hist
    tailx_ref[...] = new_hist

    gate = _spatial_gating(v, ws_ref, bias_ref, cfg)
    half = D_B // 2
    xcb = xc.astype(BF16)
    g0 = jnp.dot(xcb[:, :half], wgate_ref[0], preferred_element_type=F32)
    g1 = jnp.dot(xcb[:, half:], wgate_ref[1], preferred_element_type=F32)
    if cfg.skewed:
        xo_ref[...] = _mlp_down(x1_prev, hid_prev, down_ref, gfpost_ref)

    a_out = u * gate
    r = jax.nn.sigmoid(jnp.concatenate([g0[:, :half], g1[:, :half]], axis=1) + ba_ref[...])
    ig = jax.nn.sigmoid(jnp.concatenate([g0[:, half:], g1[:, half:]], axis=1) + bx_ref[...])
    nl = -lam_ref[...]
    softplus = jnp.maximum(nl, 0.0) + jnp.log1p(jnp.exp(-jnp.abs(nl)))
    log_a = -LRU_C * r * softplus
    a = jnp.exp(log_a)
    bterm = jnp.sqrt(1.0 - a * a) * ig * xc
    h_prev = h_scr[...]
    hseq = _linear_scan(a, bterm, h_prev, cfg)
    new_h = _keep_state(cfg, _seq_tails(hseq, cfg), h_prev)
    h_scr[...] = new_h
    tailh_ref[...] = new_h

    b_out = hseq * jax.nn.gelu(gb)
    mix = jnp.concatenate([a_out, b_out], axis=1).astype(BF16)
    m = jnp.dot(mix, wout_ref[...], preferred_element_type=F32)
    x1 = x + _rmsnorm(m, gpost_ref[...])
    if cfg.skewed:
        x1_scr[...] = x1
    else:
        xo_ref[...] = _mlp_down(x1, _mlp_up(x1, gfpre_ref, up_ref), down_ref, gfpost_ref)


def _odd_layer_kernel(cfg, x_ref, gpre_ref, win_ref, convw_ref, wout_ref, gpost_ref, gfpre_ref, up_ref,
                      down_ref, gfpost_ref, hist_in_ref, xo_ref, tail_ref, hist_scr, *rest):
    x1_scr = rest[0] if cfg.skewed else None
    _load_states_at_sequence_start(cfg, [(hist_scr, hist_in_ref)], x1_scr)

    x = x_ref[...]
    hn = _rmsnorm(x, gpre_ref[...]).astype(BF16)
    z = jnp.dot(hn, win_ref[...], preferred_element_type=F32)
    if cfg.skewed:
        x1_prev = x1_scr[...]
        xo_ref[...] = _mlp_down(x1_prev, _mlp_up(x1_prev, gfpre_ref, up_ref), down_ref, gfpost_ref)

    bg = z[:, 0:D_C]
    p = z[:, D_C:2 * D_C] * z[:, 2 * D_C:]
    hist = hist_scr[...]
    y = _causal_conv(p, hist, convw_ref[...], cfg)
    new_hist = _keep_state(cfg, _seq_tails(p, cfg), hist)
    hist_scr[...] = new_hist
    tail_ref[...] = new_hist
    m = jnp.dot((bg * y).astype(BF16), wout_ref[...], preferred_element_type=F32)
    x1 = x + _rmsnorm(m, gpost_ref[...])
    if cfg.skewed:
        x1_scr[...] = x1
    else:
        xo_ref[...] = _mlp_down(x1, _mlp_up(x1, gfpre_ref, up_ref), down_ref, gfpost_ref)


def _resident(arr):
    nd = arr.ndim
    return pl.BlockSpec(arr.shape, lambda i, _nd=nd: (0,) * _nd, pipeline_mode=pl.Buffered(1))


def _mixer_tile(cfg, i):
    return jnp.minimum(i, cfg.n_tiles - 1)


def _state_spec(cfg, width):
    return pl.BlockSpec((cfg.n_seq * SUBLANES, width),
                        lambda i: (_mixer_tile(cfg, i) // cfg.tiles_per_seq, 0))


def _row_in_spec(cfg, width):
    return pl.BlockSpec((ROW_TILE, width), lambda i: (_mixer_tile(cfg, i), 0))


def _row_out_spec(cfg, width):
    if cfg.skewed:
        return pl.BlockSpec((ROW_TILE, width), lambda i: (jnp.maximum(i - 1, 0), 0))
    return pl.BlockSpec((ROW_TILE, width), lambda i: (i, 0))


def _grid(cfg):
    return (cfg.n_tiles + 1,) if cfg.skewed else (cfg.n_tiles,)


def _compiler_params():
    return pltpu.CompilerParams(dimension_semantics=("arbitrary",), vmem_limit_bytes=VMEM_LIMIT_BYTES)


def _even_layer(cfg, x, hist, h0, w):
    rows = x.shape[0]
    n_state_rows = hist.shape[0]
    consts = [w["gpre"], w["win"], w["vgain"], w["ws"], w["bias"], w["convw"], w["convb"], w["wgate"],
              w["ba"], w["bx"], w["lam"], w["wout"], w["gpost"], w["gfpre"], w["up"], w["down"], w["gfpost"]]
    in_specs = ([_row_in_spec(cfg, D_MODEL)] + [_resident(c) for c in consts]
                + [_state_spec(cfg, D_B), _state_spec(cfg, D_B)])
    out_shape = [jax.ShapeDtypeStruct((rows, D_MODEL), F32),
                 jax.ShapeDtypeStruct((n_state_rows, D_B), F32),
                 jax.ShapeDtypeStruct((n_state_rows, D_B), F32)]
    out_specs = [_row_out_spec(cfg, D_MODEL), _state_spec(cfg, D_B), _state_spec(cfg, D_B)]
    if cfg.emit_v:
        out_shape.append(jax.ShapeDtypeStruct((rows, D_A), F32))
        out_specs.append(_row_in_spec(cfg, D_A))
    scratch = [pltpu.VMEM((cfg.n_seq * SUBLANES, D_B), F32), pltpu.VMEM((cfg.n_seq * SUBLANES, D_B), F32)]
    if cfg.skewed:
        scratch.append(pltpu.VMEM((ROW_TILE, D_MODEL), F32))
    return pl.pallas_call(
        functools.partial(_even_layer_kernel, cfg),
        grid=_grid(cfg),
        in_specs=in_specs,
        out_specs=out_specs,
        out_shape=out_shape,
        scratch_shapes=scratch,
        compiler_params=_compiler_params(),
        name="even_layer",
    )(x, *consts, hist, h0)


def _odd_layer(cfg, x, hist, w):
    rows = x.shape[0]
    consts = [w["gpre"], w["win"], w["convw"], w["wout"], w["gpost"], w["gfpre"], w["up"], w["down"],
              w["gfpost"]]
    in_specs = [_row_in_spec(cfg, D_MODEL)] + [_resident(c) for c in consts] + [_state_spec(cfg, D_C)]
    out_shape = [jax.ShapeDtypeStruct((rows, D_MODEL), F32),
                 jax.ShapeDtypeStruct((hist.shape[0], D_C), F32)]
    out_specs = [_row_out_spec(cfg, D_MODEL), _state_spec(cfg, D_C)]
    scratch = [pltpu.VMEM((cfg.n_seq * SUBLANES, D_C), F32)]
    if cfg.skewed:
        scratch.append(pltpu.VMEM((ROW_TILE, D_MODEL), F32))
    return pl.pallas_call(
        functools.partial(_odd_layer_kernel, cfg),
        grid=_grid(cfg),
        in_specs=in_specs,
        out_specs=out_specs,
        out_shape=out_shape,
        scratch_shapes=scratch,
        compiler_params=_compiler_params(),
        name="odd_layer",
    )(x, *consts, hist)


def _block_diag_gates(wa, wx):
    heads_per_half = B_HEADS // 2
    eye = jnp.eye(heads_per_half, dtype=wa.dtype)

    def bd(w4):
        return jnp.einsum("hij,hg->higj", w4, eye).reshape(D_B // 2, D_B // 2)

    halves = [jnp.concatenate([bd(wa[q * heads_per_half:(q + 1) * heads_per_half]),
                               bd(wx[q * heads_per_half:(q + 1) * heads_per_half])], axis=1)
              for q in range(2)]
    return jnp.stack(halves).astype(BF16)


def _pad_state(rows):
    n, r, c = rows.shape
    return jnp.pad(rows, ((0, 0), (SUBLANES - r, 0), (0, 0))).reshape(n * SUBLANES, c)


def _layer_weights(p, l):
    common = dict(gpre=p["norm_mix_pre"][l][None], gpost=p["norm_mix_post"][l][None],
                  gfpre=p["norm_ffn_pre"][l][None], gfpost=p["norm_ffn_post"][l][None],
                  up=p["mlp_up"][l].astype(BF16), down=p["mlp_down"][l].astype(BF16))
    if l % 2 == 0:
        e = l // 2
        a_b_s = p["a_b_s"][e]
        return dict(common, win=p["w_in_even"][e].astype(BF16), vgain=p["a_v_gain"][e][None],
                    ws=p["a_w_s"][e], bias=jnp.broadcast_to(a_b_s[..., None], a_b_s.shape + (A_HEAD_DIM,)),
                    convw=p["b_conv_w"][e], convb=p["b_conv_b"][e][None],
                    wgate=_block_diag_gates(p["b_wa"][e], p["b_wx"][e]), ba=p["b_ba"][e][None],
                    bx=p["b_bx"][e][None], lam=p["b_lambda"][e][None], wout=p["w_out_even"][e].astype(BF16))
    o = l // 2
    return dict(common, win=p["c_w_in"][o].astype(BF16), convw=p["c_conv_w"][o],
                wout=p["c_w_out"][o].astype(BF16))


def _trunk(x, b_conv, b_h, c_conv, weights, cfg):
    tails_b, tails_h, tails_c, v_rows = [], [], [], []
    for l in range(DEPTH):
        if l % 2 == 0:
            outs = _even_layer(cfg, x, b_conv[l // 2], b_h[l // 2], weights[l])
            x = outs[0]
            tails_b.append(outs[1])
            tails_h.append(outs[2])
            if cfg.emit_v:
                v_rows.append(outs[3])
        else:
            x, tail = _odd_layer(cfg, x, c_conv[l // 2], weights[l])
            tails_c.append(tail)
    return x, tails_b, tails_h, tails_c, v_rows


def kernel(x_prompt, x_sample, cache_b_conv, state_b_h, cache_c_conv, norm_mix_pre, norm_mix_post, norm_ffn_pre, norm_ffn_post, w_in_even, a_v_gain, a_w_s, a_b_s, b_conv_w, b_conv_b, b_wa, b_ba, b_wx, b_bx, b_lambda, w_out_even, c_w_in, c_conv_w, c_w_out, mlp_up, mlp_down):
    batch, seq, _ = x_prompt.shape
    dec_batch, dec_seq, _ = x_sample.shape
    n_even, n_odd = w_in_even.shape[0], c_w_in.shape[0]
    p = dict(norm_mix_pre=norm_mix_pre, norm_mix_post=norm_mix_post, norm_ffn_pre=norm_ffn_pre,
             norm_ffn_post=norm_ffn_post, w_in_even=w_in_even, a_v_gain=a_v_gain, a_w_s=a_w_s, a_b_s=a_b_s,
             b_conv_w=b_conv_w, b_conv_b=b_conv_b, b_wa=b_wa, b_wx=b_wx, b_ba=b_ba, b_bx=b_bx,
             b_lambda=b_lambda, w_out_even=w_out_even, c_w_in=c_w_in, c_conv_w=c_conv_w, c_w_out=c_w_out,
             mlp_up=mlp_up, mlp_down=mlp_down)
    weights = [_layer_weights(p, l) for l in range(DEPTH)]

    cfg_p = TileCfg(n_seq=1, seq_rows=ROW_TILE, tiles_per_seq=seq // ROW_TILE, gate_chunk=A_CHUNK,
                    emit_v=False, n_tiles=batch * seq // ROW_TILE, skewed=True)
    zb = jnp.zeros((n_even, batch * SUBLANES, D_B), F32)
    zc = jnp.zeros((n_odd, batch * SUBLANES, D_C), F32)
    y_p, tb_p, th_p, tc_p, _ = _trunk(x_prompt.reshape(batch * seq, D_MODEL), zb, zb, zc, weights, cfg_p)

    cfg_s = TileCfg(n_seq=ROW_TILE // dec_seq, seq_rows=dec_seq, tiles_per_seq=1, gate_chunk=dec_seq,
                    emit_v=True, n_tiles=dec_batch * dec_seq // ROW_TILE, skewed=False)
    sb = jnp.stack([_pad_state(cache_b_conv[e]) for e in range(n_even)])
    sh = jnp.stack([_pad_state(state_b_h[e][:, None, :]) for e in range(n_even)])
    sc = jnp.stack([_pad_state(cache_c_conv[o]) for o in range(n_odd)])
    y_s, tb_s, th_s, tc_s, v_s = _trunk(x_sample.reshape(dec_batch * dec_seq, D_MODEL), sb, sh, sc, weights,
                                         cfg_s)

    def tails(ts, n, keep):
        t = jnp.stack(ts).reshape(len(ts), n, SUBLANES, -1)
        return t[:, :, SUBLANES - keep:, :]

    return (y_p.reshape(batch, seq, D_MODEL),
            y_s.reshape(dec_batch, dec_seq, D_MODEL),
            jnp.stack(v_s).reshape(n_even, dec_batch, dec_seq, D_A),
            tails(tb_p, batch, B_CONV - 1),
            tails(th_p, batch, 1)[:, :, 0, :],
            tails(tc_p, batch, C_CONV - 1),
            tails(tb_s, dec_batch, B_CONV - 1),
            tails(th_s, dec_batch, 1)[:, :, 0, :],
            tails(tc_s, dec_batch, C_CONV - 1))
```

```python
import functools
from typing import NamedTuple

import jax
import jax.numpy as jnp
from jax import lax
from jax.experimental import pallas as pl
from jax.experimental.pallas import tpu as pltpu

D_MODEL = 1024
DEPTH = 4
CHUNK = 64
A_CHUNK = 128
D_A = D_MODEL // 2
A_HEADS = 4
A_HEAD_DIM = D_A // A_HEADS
D_B = D_MODEL // 2
B_HEADS = 8
B_HEAD_DIM = D_B // B_HEADS
B_CONV = 4
LRU_C = 8.0
D_C = D_MODEL
C_CONV = 3
D_FF = 4 * D_MODEL
EPS = 1e-6

SUBLANES = 8
V7X_VMEM_BYTES = 64 * 1024 * 1024
VMEM_LIMIT_BYTES = V7X_VMEM_BYTES - 8 * 1024 * 1024
ROW_TILE = 512

F32 = jnp.float32
BF16 = jnp.bfloat16


class TileCfg(NamedTuple):
    n_seq: int
    seq_rows: int
    tiles_per_seq: int
    gate_chunk: int
    emit_v: bool
    n_tiles: int
    skewed: bool


def _rmsnorm(x, g):
    ms = jnp.mean(x * x, axis=-1, keepdims=True)
    return x * lax.rsqrt(ms + EPS) * g


def _shift_rows(x, hist, k, cfg):
    if k == 0:
        return x
    rolled = pltpu.roll(x, k, 0)
    row = lax.broadcasted_iota(jnp.int32, (SUBLANES, x.shape[1]), 0)
    pieces = []
    for s in range(cfg.n_seq):
        lo = s * cfg.seq_rows
        h = pltpu.roll(hist[s * SUBLANES:(s + 1) * SUBLANES], k, 0)
        pieces.append(jnp.where(row < k, h, rolled[lo:lo + SUBLANES]))
        pieces.append(rolled[lo + SUBLANES:lo + cfg.seq_rows])
    return jnp.concatenate(pieces, axis=0)


def _causal_conv(x, hist, w, cfg):
    width = w.shape[0]
    y = w[0:1] * _shift_rows(x, hist, width - 1, cfg)
    for k in range(1, width):
        y = y + w[k:k + 1] * _shift_rows(x, hist, width - 1 - k, cfg)
    return y


def _seq_tails(x, cfg):
    pieces = [x[(s + 1) * cfg.seq_rows - SUBLANES:(s + 1) * cfg.seq_rows] for s in range(cfg.n_seq)]
    return pieces[0] if len(pieces) == 1 else jnp.concatenate(pieces, axis=0)


def _linear_scan(a, b, h_prev, cfg):
    row = lax.broadcasted_iota(jnp.int32, a.shape, 0) & (SUBLANES - 1)
    d = 1
    while d < SUBLANES:
        keep = row >= d
        a_sh = pltpu.roll(a, d, 0)
        b_sh = pltpu.roll(b, d, 0)
        b = jnp.where(keep, a * b_sh + b, b)
        a = jnp.where(keep, a * a_sh, a)
        d *= 2
    groups_per_seq = cfg.seq_rows // SUBLANES
    out = []
    for s in range(cfg.n_seq):
        carry = h_prev[s * SUBLANES + SUBLANES - 1:(s + 1) * SUBLANES]
        for g in range(groups_per_seq):
            lo = s * cfg.seq_rows + g * SUBLANES
            hg = b[lo:lo + SUBLANES] + a[lo:lo + SUBLANES] * carry
            out.append(hg)
            carry = hg[SUBLANES - 1:SUBLANES]
    return jnp.concatenate(out, axis=0)


def _spatial_gating(v, ws_ref, bias_ref, cfg):
    lc = cfg.gate_chunk
    n_chunks = v.shape[0] // lc
    vb = v.astype(BF16)
    pi = lax.shift_right_logical(lax.broadcasted_iota(jnp.int32, (lc, lc), 0), 6)
    pj = lax.shift_right_logical(lax.broadcasted_iota(jnp.int32, (lc, lc), 1), 6)
    mask = pj <= pi
    heads = []
    for h in range(A_HEADS):
        lanes = slice(h * A_HEAD_DIM, (h + 1) * A_HEAD_DIM)
        w = jnp.where(mask, ws_ref[h, 0:lc, 0:lc], 0.0).astype(BF16)
        vh = jnp.concatenate([vb[c * lc:(c + 1) * lc, lanes] for c in range(n_chunks)], axis=1)
        sh = jnp.dot(w, vh, preferred_element_type=F32)
        bias = bias_ref[h, 0:lc, :]
        heads.append(jnp.concatenate(
            [sh[:, c * A_HEAD_DIM:(c + 1) * A_HEAD_DIM] + bias for c in range(n_chunks)], axis=0))
    return jnp.concatenate(heads, axis=1)


def _mlp_up(x, gpre_ref, up_ref):
    hn = _rmsnorm(x, gpre_ref[...]).astype(BF16)
    return jnp.dot(hn, up_ref[...], preferred_element_type=F32)


def _mlp_down(x, hid, down_ref, gpost_ref):
    act = jnp.square(jnp.maximum(hid, 0.0)).astype(BF16)
    f = jnp.dot(act, down_ref[...], preferred_element_type=F32)
    return x + _rmsnorm(f, gpost_ref[...])


def _tile_is_real(cfg):
    return pl.program_id(0) < cfg.n_tiles


def _load_states_at_sequence_start(cfg, pairs, x1_scr):
    i = pl.program_id(0)

    @pl.when(jnp.logical_and(lax.rem(i, cfg.tiles_per_seq) == 0, _tile_is_real(cfg)))
    def _():
        for scr, src in pairs:
            scr[...] = src[...]

    if cfg.skewed:
        @pl.when(i == 0)
        def _():
            x1_scr[...] = jnp.zeros_like(x1_scr)


def _keep_state(cfg, new, old):
    return jnp.where(_tile_is_real(cfg), new, old) if cfg.skewed else new


def _even_layer_kernel(cfg, x_ref, gpre_ref, win_ref, vgain_ref, ws_ref, bias_ref, convw_ref, convb_ref,
                       wgate_ref, ba_ref, bx_ref, lam_ref, wout_ref, gpost_ref, gfpre_ref, up_ref,
                       down_ref, gfpost_ref, hist_in_ref, h_in_ref, *rest):
    rest = list(rest)
    xo_ref, tailx_ref, tailh_ref = rest[:3]
    v_ref = rest[3] if cfg.emit_v else None
    hist_scr, h_scr = rest[-3:-1] if cfg.skewed else rest[-2:]
    x1_scr = rest[-1] if cfg.skewed else None

    _load_states_at_sequence_start(cfg, [(hist_scr, hist_in_ref), (h_scr, h_in_ref)], x1_scr)

    x = x_ref[...]
    hn = _rmsnorm(x, gpre_ref[...]).astype(BF16)
    z = jnp.dot(hn, win_ref[...], preferred_element_type=F32)
    if cfg.skewed:
        x1_prev = x1_scr[...]
        hid_prev = _mlp_up(x1_prev, gfpre_ref, up_ref)

    u = jax.nn.gelu(z[:, 0:D_A])
    v = _rmsnorm(jax.nn.gelu(z[:, D_A:2 * D_A]), vgain_ref[...])
    xb = z[:, 2 * D_A:2 * D_A + D_B]
    gb = z[:, 2 * D_A + D_B:]
    if v_ref is not None:
        v_ref[...] = v
    hist = hist_scr[...]
    xc = _causal_conv(xb, hist, convw_ref[...], cfg) + convb_ref[...]
    new_hist = _keep_state(cfg, _seq_tails(xb, cfg), hist)
    hist_scr[...] = new_hist
    tailx_ref[...] = new_hist

    gate = _spatial_gating(v, ws_ref, bias_ref, cfg)
    half = D_B // 2
    xcb = xc.astype(BF16)
    g0 = jnp.dot(xcb[:, :half], wgate_ref[0], preferred_element_type=F32)
    g1 = jnp.dot(xcb[:, half:], wgate_ref[1], preferred_element_type=F32)
    if cfg.skewed:
        xo_ref[...] = _mlp_down(x1_prev, hid_prev, down_ref, gfpost_ref)

    a_out = u * gate
    r = jax.nn.sigmoid(jnp.concatenate([g0[:, :half], g1[:, :half]], axis=1) + ba_ref[...])
    ig = jax.nn.sigmoid(jnp.concatenate([g0[:, half:], g1[:, half:]], axis=1) + bx_ref[...])
    nl = -lam_ref[...]
    softplus = jnp.maximum(nl, 0.0) + jnp.log1p(jnp.exp(-jnp.abs(nl)))
    log_a = -LRU_C * r * softplus
    a = jnp.exp(log_a)
    bterm = jnp.sqrt(1.0 - a * a) * ig * xc
    h_prev = h_scr[...]
    hseq = _linear_scan(a, bterm, h_prev, cfg)
    new_h = _keep_state(cfg, _seq_tails(hseq, cfg), h_prev)
    h_scr[...] = new_h
    tailh_ref[...] = new_h

    b_out = hseq * jax.nn.gelu(gb)
    mix = jnp.concatenate([a_out, b_out], axis=1).astype(BF16)
    m = jnp.dot(mix, wout_ref[...], preferred_element_type=F32)
    x1 = x + _rmsnorm(m, gpost_ref[...])
    if cfg.skewed:
        x1_scr[...] = x1
    else:
        xo_ref[...] = _mlp_down(x1, _mlp_up(x1, gfpre_ref, up_ref), down_ref, gfpost_ref)


def _odd_layer_kernel(cfg, x_ref, gpre_ref, win_ref, convw_ref, wout_ref, gpost_ref, gfpre_ref, up_ref,
                      down_ref, gfpost_ref, hist_in_ref, xo_ref, tail_ref, hist_scr, *rest):
    x1_scr = rest[0] if cfg.skewed else None
    _load_states_at_sequence_start(cfg, [(hist_scr, hist_in_ref)], x1_scr)

    x = x_ref[...]
    hn = _rmsnorm(x, gpre_ref[...]).astype(BF16)
    z = jnp.dot(hn, win_ref[...], preferred_element_type=F32)
    if cfg.skewed:
        x1_prev = x1_scr[...]
        xo_ref[...] = _mlp_down(x1_prev, _mlp_up(x1_prev, gfpre_ref, up_ref), down_ref, gfpost_ref)

    bg = z[:, 0:D_C]
    p = z[:, D_C:2 * D_C] * z[:, 2 * D_C:]
    hist = hist_scr[...]
    y = _causal_conv(p, hist, convw_ref[...], cfg)
    new_hist = _keep_state(cfg, _seq_tails(p, cfg), hist)
    hist_scr[...] = new_hist
    tail_ref[...] = new_hist
    m = jnp.dot((bg * y).astype(BF16), wout_ref[...], preferred_element_type=F32)
    x1 = x + _rmsnorm(m, gpost_ref[...])
    if cfg.skewed:
        x1_scr[...] = x1
    else:
        xo_ref[...] = _mlp_down(x1, _mlp_up(x1, gfpre_ref, up_ref), down_ref, gfpost_ref)


def _resident(arr):
    nd = arr.ndim
    return pl.BlockSpec(arr.shape, lambda i, _nd=nd: (0,) * _nd, pipeline_mode=pl.Buffered(1))


def _mixer_tile(cfg, i):
    return jnp.minimum(i, cfg.n_tiles - 1)


def _state_spec(cfg, width):
    return pl.BlockSpec((cfg.n_seq * SUBLANES, width),
                        lambda i: (_mixer_tile(cfg, i) // cfg.tiles_per_seq, 0))


def _row_in_spec(cfg, width):
    return pl.BlockSpec((ROW_TILE, width), lambda i: (_mixer_tile(cfg, i), 0))


def _row_out_spec(cfg, width):
    if cfg.skewed:
        return pl.BlockSpec((ROW_TILE, width), lambda i: (jnp.maximum(i - 1, 0), 0))
    return pl.BlockSpec((ROW_TILE, width), lambda i: (i, 0))


def _grid(cfg):
    return (cfg.n_tiles + 1,) if cfg.skewed else (cfg.n_tiles,)


def _compiler_params():
    return pltpu.CompilerParams(dimension_semantics=("arbitrary",), vmem_limit_bytes=VMEM_LIMIT_BYTES)


def _even_layer(cfg, x, hist, h0, w):
    rows = x.shape[0]
    n_state_rows = hist.shape[0]
    consts = [w["gpre"], w["win"], w["vgain"], w["ws"], w["bias"], w["convw"], w["convb"], w["wgate"],
              w["ba"], w["bx"], w["lam"], w["wout"], w["gpost"], w["gfpre"], w["up"], w["down"], w["gfpost"]]
    in_specs = ([_row_in_spec(cfg, D_MODEL)] + [_resident(c) for c in consts]
                + [_state_spec(cfg, D_B), _state_spec(cfg, D_B)])
    out_shape = [jax.ShapeDtypeStruct((rows, D_MODEL), F32),
                 jax.ShapeDtypeStruct((n_state_rows, D_B), F32),
                 jax.ShapeDtypeStruct((n_state_rows, D_B), F32)]
    out_specs = [_row_out_spec(cfg, D_MODEL), _state_spec(cfg, D_B), _state_spec(cfg, D_B)]
    if cfg.emit_v:
        out_shape.append(jax.ShapeDtypeStruct((rows, D_A), F32))
        out_specs.append(_row_in_spec(cfg, D_A))
    scratch = [pltpu.VMEM((cfg.n_seq * SUBLANES, D_B), F32), pltpu.VMEM((cfg.n_seq * SUBLANES, D_B), F32)]
    if cfg.skewed:
        scratch.append(pltpu.VMEM((ROW_TILE, D_MODEL), F32))
    return pl.pallas_call(
        functools.partial(_even_layer_kernel, cfg),
        grid=_grid(cfg),
        in_specs=in_specs,
        out_specs=out_specs,
        out_shape=out_shape,
        scratch_shapes=scratch,
        compiler_params=_compiler_params(),
        name="even_layer",
    )(x, *consts, hist, h0)


def _odd_layer(cfg, x, hist, w):
    rows = x.shape[0]
    consts = [w["gpre"], w["win"], w["convw"], w["wout"], w["gpost"], w["gfpre"], w["up"], w["down"],
              w["gfpost"]]
    in_specs = [_row_in_spec(cfg, D_MODEL)] + [_resident(c) for c in consts] + [_state_spec(cfg, D_C)]
    out_shape = [jax.ShapeDtypeStruct((rows, D_MODEL), F32),
                 jax.ShapeDtypeStruct((hist.shape[0], D_C), F32)]
    out_specs = [_row_out_spec(cfg, D_MODEL), _state_spec(cfg, D_C)]
    scratch = [pltpu.VMEM((cfg.n_seq * SUBLANES, D_C), F32)]
    if cfg.skewed:
        scratch.append(pltpu.VMEM((ROW_TILE, D_MODEL), F32))
    return pl.pallas_call(
        functools.partial(_odd_layer_kernel, cfg),
        grid=_grid(cfg),
        in_specs=in_specs,
        out_specs=out_specs,
        out_shape=out_shape,
        scratch_shapes=scratch,
        compiler_params=_compiler_params(),
        name="odd_layer",
    )(x, *consts, hist)


def _block_diag_gates(wa, wx):
    heads_per_half = B_HEADS // 2
    eye = jnp.eye(heads_per_half, dtype=wa.dtype)

    def bd(w4):
        return jnp.einsum("hij,hg->higj", w4, eye).reshape(D_B // 2, D_B // 2)

    halves = [jnp.concatenate([bd(wa[q * heads_per_half:(q + 1) * heads_per_half]),
                               bd(wx[q * heads_per_half:(q + 1) * heads_per_half])], axis=1)
              for q in range(2)]
    return jnp.stack(halves).astype(BF16)


def _pad_state(rows):
    n, r, c = rows.shape
    return jnp.pad(rows, ((0, 0), (SUBLANES - r, 0), (0, 0))).reshape(n * SUBLANES, c)


def _layer_weights(p, l):
    common = dict(gpre=p["norm_mix_pre"][l][None], gpost=p["norm_mix_post"][l][None],
                  gfpre=p["norm_ffn_pre"][l][None], gfpost=p["norm_ffn_post"][l][None],
                  up=p["mlp_up"][l].astype(BF16), down=p["mlp_down"][l].astype(BF16))
    if l % 2 == 0:
        e = l // 2
        a_b_s = p["a_b_s"][e]
        return dict(common, win=p["w_in_even"][e].astype(BF16), vgain=p["a_v_gain"][e][None],
                    ws=p["a_w_s"][e], bias=jnp.broadcast_to(a_b_s[..., None], a_b_s.shape + (A_HEAD_DIM,)),
                    convw=p["b_conv_w"][e], convb=p["b_conv_b"][e][None],
                    wgate=_block_diag_gates(p["b_wa"][e], p["b_wx"][e]), ba=p["b_ba"][e][None],
                    bx=p["b_bx"][e][None], lam=p["b_lambda"][e][None], wout=p["w_out_even"][e].astype(BF16))
    o = l // 2
    return dict(common, win=p["c_w_in"][o].astype(BF16), convw=p["c_conv_w"][o],
                wout=p["c_w_out"][o].astype(BF16))


def _trunk(x, b_conv, b_h, c_conv, weights, cfg):
    tails_b, tails_h, tails_c, v_rows = [], [], [], []
    for l in range(DEPTH):
        if l % 2 == 0:
            outs = _even_layer(cfg, x, b_conv[l // 2], b_h[l // 2], weights[l])
            x = outs[0]
            tails_b.append(outs[1])
            tails_h.append(outs[2])
            if cfg.emit_v:
                v_rows.append(outs[3])
        else:
            x, tail = _odd_layer(cfg, x, c_conv[l // 2], weights[l])
            tails_c.append(tail)
    return x, tails_b, tails_h, tails_c, v_rows


def kernel(x_prompt, x_sample, cache_b_conv, state_b_h, cache_c_conv, norm_mix_pre, norm_mix_post, norm_ffn_pre, norm_ffn_post, w_in_even, a_v_gain, a_w_s, a_b_s, b_conv_w, b_conv_b, b_wa, b_ba, b_wx, b_bx, b_lambda, w_out_even, c_w_in, c_conv_w, c_w_out, mlp_up, mlp_down):
    batch, seq, _ = x_prompt.shape
    dec_batch, dec_seq, _ = x_sample.shape
    n_even, n_odd = w_in_even.shape[0], c_w_in.shape[0]
    p = dict(norm_mix_pre=norm_mix_pre, norm_mix_post=norm_mix_post, norm_ffn_pre=norm_ffn_pre,
             norm_ffn_post=norm_ffn_post, w_in_even=w_in_even, a_v_gain=a_v_gain, a_w_s=a_w_s, a_b_s=a_b_s,
             b_conv_w=b_conv_w, b_conv_b=b_conv_b, b_wa=b_wa, b_wx=b_wx, b_ba=b_ba, b_bx=b_bx,
             b_lambda=b_lambda, w_out_even=w_out_even, c_w_in=c_w_in, c_conv_w=c_conv_w, c_w_out=c_w_out,
             mlp_up=mlp_up, mlp_down=mlp_down)
    weights = [_layer_weights(p, l) for l in range(DEPTH)]

    cfg_p = TileCfg(n_seq=1, seq_rows=ROW_TILE, tiles_per_seq=seq // ROW_TILE, gate_chunk=A_CHUNK,
                    emit_v=False, n_tiles=batch * seq // ROW_TILE, skewed=True)
    zb = jnp.zeros((n_even, batch * SUBLANES, D_B), F32)
    zc = jnp.zeros((n_odd, batch * SUBLANES, D_C), F32)
    y_p, tb_p, th_p, tc_p, _ = _trunk(x_prompt.reshape(batch * seq, D_MODEL), zb, zb, zc, weights, cfg_p)

    cfg_s = TileCfg(n_seq=ROW_TILE // dec_seq, seq_rows=dec_seq, tiles_per_seq=1, gate_chunk=dec_seq,
                    emit_v=True, n_tiles=dec_batch * dec_seq // ROW_TILE, skewed=False)
    sb = jnp.stack([_pad_state(cache_b_conv[e]) for e in range(n_even)])
    sh = jnp.stack([_pad_state(state_b_h[e][:, None, :]) for e in range(n_even)])
    sc = jnp.stack([_pad_state(cache_c_conv[o]) for o in range(n_odd)])
    y_s, tb_s, th_s, tc_s, v_s = _trunk(x_sample.reshape(dec_batch * dec_seq, D_MODEL), sb, sh, sc, weights,
                                         cfg_s)

    def tails(ts, n, keep):
        t = jnp.stack(ts).reshape(len(ts), n, SUBLANES, -1)
        return t[:, :, SUBLANES - keep:, :]

    return (y_p.reshape(batch, seq, D_MODEL),
            y_s.reshape(dec_batch, dec_seq, D_MODEL),
            jnp.stack(v_s).reshape(n_even, dec_batch, dec_seq, D_A),
            tails(tb_p, batch, B_CONV - 1),
            tails(th_p, batch, 1)[:, :, 0, :],
            tails(tc_p, batch, C_CONV - 1),
            tails(tb_s, dec_batch, B_CONV - 1),
            tails(th_s, dec_batch, 1)[:, :, 0, :],
            tails(tc_s, dec_batch, C_CONV - 1))
```

```python
import functools
from typing import NamedTuple

import jax
import jax.numpy as jnp
from jax import lax
from jax.experimental import pallas as pl
from jax.experimental.pallas import tpu as pltpu

D_MODEL = 1024
DEPTH = 4
CHUNK = 64
A_CHUNK = 128
D_A = D_MODEL // 2
A_HEADS = 4
A_HEAD_DIM = D_A // A_HEADS
D_B = D_MODEL // 2
B_HEADS = 8
B_HEAD_DIM = D_B // B_HEADS
B_CONV = 4
LRU_C = 8.0
D_C = D_MODEL
C_CONV = 3
D_FF = 4 * D_MODEL
EPS = 1e-6

SUBLANES = 8
V7X_VMEM_BYTES = 64 * 1024 * 1024
VMEM_LIMIT_BYTES = V7X_VMEM_BYTES - 8 * 1024 * 1024
ROW_TILE = 512

F32 = jnp.float32
BF16 = jnp.bfloat16


class TileCfg(NamedTuple):
    n_seq: int
    seq_rows: int
    tiles_per_seq: int
    gate_chunk: int
    emit_v: bool
    n_tiles: int
    skewed: bool


def _rmsnorm(x, g):
    ms = jnp.mean(x * x, axis=-1, keepdims=True)
    return x * lax.rsqrt(ms + EPS) * g


def _shift_rows(x, hist, k, cfg):
    if k == 0:
        return x
    rolled = pltpu.roll(x, k, 0)
    row = lax.broadcasted_iota(jnp.int32, (SUBLANES, x.shape[1]), 0)
    pieces = []
    for s in range(cfg.n_seq):
        lo = s * cfg.seq_rows
        h = pltpu.roll(hist[s * SUBLANES:(s + 1) * SUBLANES], k, 0)
        pieces.append(jnp.where(row < k, h, rolled[lo:lo + SUBLANES]))
        pieces.append(rolled[lo + SUBLANES:lo + cfg.seq_rows])
    return jnp.concatenate(pieces, axis=0)


def _causal_conv(x, hist, w, cfg):
    width = w.shape[0]
    y = w[0:1] * _shift_rows(x, hist, width - 1, cfg)
    for k in range(1, width):
        y = y + w[k:k + 1] * _shift_rows(x, hist, width - 1 - k, cfg)
    return y


def _seq_tails(x, cfg):
    pieces = [x[(s + 1) * cfg.seq_rows - SUBLANES:(s + 1) * cfg.seq_rows] for s in range(cfg.n_seq)]
    return pieces[0] if len(pieces) == 1 else jnp.concatenate(pieces, axis=0)


def _linear_scan(a, b, h_prev, cfg):
    row = lax.broadcasted_iota(jnp.int32, a.shape, 0) & (SUBLANES - 1)
    d = 1
    while d < SUBLANES:
        keep = row >= d
        a_sh = pltpu.roll(a, d, 0)
        b_sh = pltpu.roll(b, d, 0)
        b = jnp.where(keep, a * b_sh + b, b)
        a = jnp.where(keep, a * a_sh, a)
        d *= 2
    groups_per_seq = cfg.seq_rows // SUBLANES
    out = []
    for s in range(cfg.n_seq):
        carry = h_prev[s * SUBLANES + SUBLANES - 1:(s + 1) * SUBLANES]
        for g in range(groups_per_seq):
            lo = s * cfg.seq_rows + g * SUBLANES
            hg = b[lo:lo + SUBLANES] + a[lo:lo + SUBLANES] * carry
            out.append(hg)
            carry = hg[SUBLANES - 1:SUBLANES]
    return jnp.concatenate(out, axis=0)


def _spatial_gating(v, ws_ref, bias_ref, gate_scr, cfg):
    lc = cfg.gate_chunk
    n_chunks = v.shape[0] // lc
    vb = v.astype(BF16)
    pi = lax.shift_right_logical(lax.broadcasted_iota(jnp.int32, (lc, lc), 0), 6)
    pj = lax.shift_right_logical(lax.broadcasted_iota(jnp.int32, (lc, lc), 1), 6)
    mask = pj <= pi
    for h in range(A_HEADS):
        lanes = slice(h * A_HEAD_DIM, (h + 1) * A_HEAD_DIM)
        w = jnp.where(mask, ws_ref[h, 0:lc, 0:lc], 0.0).astype(BF16)
        vh = jnp.concatenate([vb[c * lc:(c + 1) * lc, lanes] for c in range(n_chunks)], axis=1)
        sh = jnp.dot(w, vh, preferred_element_type=F32)
        bias = bias_ref[h, 0:lc, :]
        for c in range(n_chunks):
            gate_scr[c * lc:(c + 1) * lc, lanes] = sh[:, c * A_HEAD_DIM:(c + 1) * A_HEAD_DIM] + bias
    return gate_scr[...]


def _bf16_rows(packed):
    return pltpu.bitcast(packed, BF16)


def _mlp_up_half(hn, up_ref, half):
    cols = D_FF // 2
    hid = jnp.dot(hn, _bf16_rows(up_ref[:, half * cols:(half + 1) * cols]), preferred_element_type=F32)
    return jnp.square(jnp.maximum(hid, 0.0)).astype(BF16)


def _mlp_down_half(act, down_ref, half):
    packed_rows = D_FF // 4
    w = _bf16_rows(down_ref[half * packed_rows:(half + 1) * packed_rows, :])
    return jnp.dot(act, w, preferred_element_type=F32)


def _mlp_residual(x, gpre_ref, up_ref, down_ref, gpost_ref):
    hn = _rmsnorm(x, gpre_ref[...]).astype(BF16)
    act0 = _mlp_up_half(hn, up_ref, 0)
    act1 = _mlp_up_half(hn, up_ref, 1)
    f = _mlp_down_half(act0, down_ref, 0) + _mlp_down_half(act1, down_ref, 1)
    return x + _rmsnorm(f, gpost_ref[...])


def _tile_is_real(cfg):
    return pl.program_id(0) < cfg.n_tiles


def _load_states_at_sequence_start(cfg, pairs, x1_scr):
    i = pl.program_id(0)

    @pl.when(jnp.logical_and(lax.rem(i, cfg.tiles_per_seq) == 0, _tile_is_real(cfg)))
    def _():
        for scr, src in pairs:
            scr[...] = src[...]

    if cfg.skewed:
        @pl.when(i == 0)
        def _():
            x1_scr[...] = jnp.zeros_like(x1_scr)


def _keep_state(cfg, new, old):
    return jnp.where(_tile_is_real(cfg), new, old) if cfg.skewed else new


def _even_layer_kernel(cfg, x_ref, gpre_ref, win_ref, vgain_ref, ws_ref, bias_ref, convw_ref, convb_ref,
                       wgate_ref, ba_ref, bx_ref, lam_ref, wout_ref, gpost_ref, gfpre_ref, up_ref,
                       down_ref, gfpost_ref, hist_in_ref, h_in_ref, *rest):
    rest = list(rest)
    xo_ref, tailx_ref, tailh_ref = rest[:3]
    v_ref = rest[3] if cfg.emit_v else None
    n_out = 4 if cfg.emit_v else 3
    hist_scr, h_scr, lru_slots, gate_slots = rest[n_out:n_out + 4]
    slot = pl.program_id(0) & 1
    lru_scr, gate_scr = lru_slots.at[slot], gate_slots.at[slot]
    x1_scr = rest[n_out + 4] if cfg.skewed else None

    _load_states_at_sequence_start(cfg, [(hist_scr, hist_in_ref), (h_scr, h_in_ref)], x1_scr)

    x = x_ref[...]
    hn = _rmsnorm(x, gpre_ref[...]).astype(BF16)
    z = jnp.dot(hn, _bf16_rows(win_ref[...]), preferred_element_type=F32)
    if cfg.skewed:
        x1_prev = x1_scr[...]
        hn_prev = _rmsnorm(x1_prev, gfpre_ref[...]).astype(BF16)
        act_prev0 = _mlp_up_half(hn_prev, up_ref, 0)

    xb = z[:, 2 * D_A:2 * D_A + D_B]
    gb = z[:, 2 * D_A + D_B:]
    hist = hist_scr[...]
    xc = _causal_conv(xb, hist, convw_ref[...], cfg) + convb_ref[...]
    new_hist = _keep_state(cfg, _seq_tails(xb, cfg), hist)
    hist_scr[...] = new_hist
    tailx_ref[...] = new_hist
    half = D_B // 2
    xcb = xc.astype(BF16)
    lru_scr[:, :D_B] = jnp.dot(xcb[:, :half], _bf16_rows(wgate_ref[0]), preferred_element_type=F32)
    lru_scr[:, D_B:] = jnp.dot(xcb[:, half:], _bf16_rows(wgate_ref[1]), preferred_element_type=F32)
    r = jax.nn.sigmoid(jnp.concatenate([lru_scr[:, 0:half], lru_scr[:, D_B:D_B + half]], axis=1) + ba_ref[...])
    ig = jax.nn.sigmoid(jnp.concatenate([lru_scr[:, half:D_B], lru_scr[:, D_B + half:]], axis=1) + bx_ref[...])
    nl = -lam_ref[...]
    softplus = jnp.maximum(nl, 0.0) + jnp.log1p(jnp.exp(-jnp.abs(nl)))
    log_a = -LRU_C * r * softplus
    a = jnp.exp(log_a)
    bterm = jnp.sqrt(1.0 - a * a) * ig * xc
    h_prev = h_scr[...]
    hseq = _linear_scan(a, bterm, h_prev, cfg)
    new_h = _keep_state(cfg, _seq_tails(hseq, cfg), h_prev)
    h_scr[...] = new_h
    tailh_ref[...] = new_h

    b_out = hseq * jax.nn.gelu(gb)
    if cfg.skewed:
        act_prev1 = _mlp_up_half(hn_prev, up_ref, 1)
        f_prev = _mlp_down_half(act_prev0, down_ref, 0) + _mlp_down_half(act_prev1, down_ref, 1)
        xo_ref[...] = x1_prev + _rmsnorm(f_prev, gfpost_ref[...])

    u = jax.nn.gelu(z[:, 0:D_A])
    v = _rmsnorm(jax.nn.gelu(z[:, D_A:2 * D_A]), vgain_ref[...])
    if v_ref is not None:
        v_ref[...] = v
    a_out = u * _spatial_gating(v, ws_ref, bias_ref, gate_scr, cfg)

    packed_a_rows = D_A // 2
    m = jnp.dot(b_out.astype(BF16), _bf16_rows(wout_ref[packed_a_rows:, :]), preferred_element_type=F32)
    m = m + jnp.dot(a_out.astype(BF16), _bf16_rows(wout_ref[:packed_a_rows, :]), preferred_element_type=F32)
    x1 = x + _rmsnorm(m, gpost_ref[...])
    if cfg.skewed:
        x1_scr[...] = x1
    else:
        xo_ref[...] = _mlp_residual(x1, gfpre_ref, up_ref, down_ref, gfpost_ref)


def _odd_layer_kernel(cfg, x_ref, gpre_ref, win_ref, convw_ref, wout_ref, gpost_ref, gfpre_ref, up_ref,
                      down_ref, gfpost_ref, hist_in_ref, xo_ref, tail_ref, hist_scr, *rest):
    x1_scr = rest[0] if cfg.skewed else None
    _load_states_at_sequence_start(cfg, [(hist_scr, hist_in_ref)], x1_scr)

    x = x_ref[...]
    hn = _rmsnorm(x, gpre_ref[...]).astype(BF16)
    z = jnp.dot(hn, _bf16_rows(win_ref[...]), preferred_element_type=F32)
    if cfg.skewed:
        xo_ref[...] = _mlp_residual(x1_scr[...], gfpre_ref, up_ref, down_ref, gfpost_ref)

    bg = z[:, 0:D_C]
    p = z[:, D_C:2 * D_C] * z[:, 2 * D_C:]
    hist = hist_scr[...]
    y = _causal_conv(p, hist, convw_ref[...], cfg)
    new_hist = _keep_state(cfg, _seq_tails(p, cfg), hist)
    hist_scr[...] = new_hist
    tail_ref[...] = new_hist
    m = jnp.dot((bg * y).astype(BF16), _bf16_rows(wout_ref[...]), preferred_element_type=F32)
    x1 = x + _rmsnorm(m, gpost_ref[...])
    if cfg.skewed:
        x1_scr[...] = x1
    else:
        xo_ref[...] = _mlp_residual(x1, gfpre_ref, up_ref, down_ref, gfpost_ref)


def _resident(arr):
    nd = arr.ndim
    return pl.BlockSpec(arr.shape, lambda i, _nd=nd: (0,) * _nd, pipeline_mode=pl.Buffered(1))


def _mixer_tile(cfg, i):
    return jnp.minimum(i, cfg.n_tiles - 1)


def _state_spec(cfg, width):
    return pl.BlockSpec((cfg.n_seq * SUBLANES, width),
                        lambda i: (_mixer_tile(cfg, i) // cfg.tiles_per_seq, 0))


def _row_in_spec(cfg, width):
    return pl.BlockSpec((ROW_TILE, width), lambda i: (_mixer_tile(cfg, i), 0))


def _row_out_spec(cfg, width):
    if cfg.skewed:
        return pl.BlockSpec((ROW_TILE, width), lambda i: (jnp.maximum(i - 1, 0), 0))
    return pl.BlockSpec((ROW_TILE, width), lambda i: (i, 0))


def _grid(cfg):
    return (cfg.n_tiles + 1,) if cfg.skewed else (cfg.n_tiles,)


def _compiler_params():
    return pltpu.CompilerParams(dimension_semantics=("arbitrary",), vmem_limit_bytes=VMEM_LIMIT_BYTES)


def _even_layer(cfg, x, hist, h0, w):
    rows = x.shape[0]
    n_state_rows = hist.shape[0]
    consts = [w["gpre"], w["win"], w["vgain"], w["ws"], w["bias"], w["convw"], w["convb"], w["wgate"],
              w["ba"], w["bx"], w["lam"], w["wout"], w["gpost"], w["gfpre"], w["up"], w["down"], w["gfpost"]]
    in_specs = ([_row_in_spec(cfg, D_MODEL)] + [_resident(c) for c in consts]
                + [_state_spec(cfg, D_B), _state_spec(cfg, D_B)])
    out_shape = [jax.ShapeDtypeStruct((rows, D_MODEL), F32),
                 jax.ShapeDtypeStruct((n_state_rows, D_B), F32),
                 jax.ShapeDtypeStruct((n_state_rows, D_B), F32)]
    out_specs = [_row_out_spec(cfg, D_MODEL), _state_spec(cfg, D_B), _state_spec(cfg, D_B)]
    if cfg.emit_v:
        out_shape.append(jax.ShapeDtypeStruct((rows, D_A), F32))
        out_specs.append(_row_in_spec(cfg, D_A))
    scratch = [pltpu.VMEM((cfg.n_seq * SUBLANES, D_B), F32), pltpu.VMEM((cfg.n_seq * SUBLANES, D_B), F32),
               pltpu.VMEM((2, ROW_TILE, 2 * D_B), F32), pltpu.VMEM((2, ROW_TILE, D_A), F32)]
    if cfg.skewed:
        scratch.append(pltpu.VMEM((ROW_TILE, D_MODEL), F32))
    return pl.pallas_call(
        functools.partial(_even_layer_kernel, cfg),
        grid=_grid(cfg),
        in_specs=in_specs,
        out_specs=out_specs,
        out_shape=out_shape,
        scratch_shapes=scratch,
        compiler_params=_compiler_params(),
        name="even_layer",
    )(x, *consts, hist, h0)


def _odd_layer(cfg, x, hist, w):
    rows = x.shape[0]
    consts = [w["gpre"], w["win"], w["convw"], w["wout"], w["gpost"], w["gfpre"], w["up"], w["down"],
              w["gfpost"]]
    in_specs = [_row_in_spec(cfg, D_MODEL)] + [_resident(c) for c in consts] + [_state_spec(cfg, D_C)]
    out_shape = [jax.ShapeDtypeStruct((rows, D_MODEL), F32),
                 jax.ShapeDtypeStruct((hist.shape[0], D_C), F32)]
    out_specs = [_row_out_spec(cfg, D_MODEL), _state_spec(cfg, D_C)]
    scratch = [pltpu.VMEM((cfg.n_seq * SUBLANES, D_C), F32)]
    if cfg.skewed:
        scratch.append(pltpu.VMEM((ROW_TILE, D_MODEL), F32))
    return pl.pallas_call(
        functools.partial(_odd_layer_kernel, cfg),
        grid=_grid(cfg),
        in_specs=in_specs,
        out_specs=out_specs,
        out_shape=out_shape,
        scratch_shapes=scratch,
        compiler_params=_compiler_params(),
        name="odd_layer",
    )(x, *consts, hist)


def _pack_rows(w):
    bits = lax.bitcast_convert_type(w.astype(BF16), jnp.uint16).astype(jnp.uint32)
    pairs = bits.reshape(bits.shape[:-2] + (bits.shape[-2] // 2, 2, bits.shape[-1]))
    return pairs[..., 0, :] | (pairs[..., 1, :] << 16)


def _block_diag_gates(wa, wx):
    heads_per_half = B_HEADS // 2
    eye = jnp.eye(heads_per_half, dtype=wa.dtype)

    def bd(w4):
        return jnp.einsum("hij,hg->higj", w4, eye).reshape(D_B // 2, D_B // 2)

    halves = [jnp.concatenate([bd(wa[q * heads_per_half:(q + 1) * heads_per_half]),
                               bd(wx[q * heads_per_half:(q + 1) * heads_per_half])], axis=1)
              for q in range(2)]
    return _pack_rows(jnp.stack(halves))


def _pad_state(rows):
    n, r, c = rows.shape
    return jnp.pad(rows, ((0, 0), (SUBLANES - r, 0), (0, 0))).reshape(n * SUBLANES, c)


def _layer_weights(p, l):
    common = dict(gpre=p["norm_mix_pre"][l][None], gpost=p["norm_mix_post"][l][None],
                  gfpre=p["norm_ffn_pre"][l][None], gfpost=p["norm_ffn_post"][l][None],
                  up=_pack_rows(p["mlp_up"][l]), down=_pack_rows(p["mlp_down"][l]))
    if l % 2 == 0:
        e = l // 2
        a_b_s = p["a_b_s"][e]
        return dict(common, win=_pack_rows(p["w_in_even"][e]), vgain=p["a_v_gain"][e][None],
                    ws=p["a_w_s"][e], bias=jnp.broadcast_to(a_b_s[..., None], a_b_s.shape + (A_HEAD_DIM,)),
                    convw=p["b_conv_w"][e], convb=p["b_conv_b"][e][None],
                    wgate=_block_diag_gates(p["b_wa"][e], p["b_wx"][e]), ba=p["b_ba"][e][None],
                    bx=p["b_bx"][e][None], lam=p["b_lambda"][e][None], wout=_pack_rows(p["w_out_even"][e]))
    o = l // 2
    return dict(common, win=_pack_rows(p["c_w_in"][o]), convw=p["c_conv_w"][o],
                wout=_pack_rows(p["c_w_out"][o]))


def _trunk(x, b_conv, b_h, c_conv, weights, cfg):
    tails_b, tails_h, tails_c, v_rows = [], [], [], []
    for l in range(DEPTH):
        if l % 2 == 0:
            outs = _even_layer(cfg, x, b_conv[l // 2], b_h[l // 2], weights[l])
            x = outs[0]
            tails_b.append(outs[1])
            tails_h.append(outs[2])
            if cfg.emit_v:
                v_rows.append(outs[3])
        else:
            x, tail = _odd_layer(cfg, x, c_conv[l // 2], weights[l])
            tails_c.append(tail)
    return x, tails_b, tails_h, tails_c, v_rows


def kernel(x_prompt, x_sample, cache_b_conv, state_b_h, cache_c_conv, norm_mix_pre, norm_mix_post, norm_ffn_pre, norm_ffn_post, w_in_even, a_v_gain, a_w_s, a_b_s, b_conv_w, b_conv_b, b_wa, b_ba, b_wx, b_bx, b_lambda, w_out_even, c_w_in, c_conv_w, c_w_out, mlp_up, mlp_down):
    batch, seq, _ = x_prompt.shape
    dec_batch, dec_seq, _ = x_sample.shape
    n_even, n_odd = w_in_even.shape[0], c_w_in.shape[0]
    p = dict(norm_mix_pre=norm_mix_pre, norm_mix_post=norm_mix_post, norm_ffn_pre=norm_ffn_pre,
             norm_ffn_post=norm_ffn_post, w_in_even=w_in_even, a_v_gain=a_v_gain, a_w_s=a_w_s, a_b_s=a_b_s,
             b_conv_w=b_conv_w, b_conv_b=b_conv_b, b_wa=b_wa, b_wx=b_wx, b_ba=b_ba, b_bx=b_bx,
             b_lambda=b_lambda, w_out_even=w_out_even, c_w_in=c_w_in, c_conv_w=c_conv_w, c_w_out=c_w_out,
             mlp_up=mlp_up, mlp_down=mlp_down)
    weights = [_layer_weights(p, l) for l in range(DEPTH)]

    cfg_p = TileCfg(n_seq=1, seq_rows=ROW_TILE, tiles_per_seq=seq // ROW_TILE, gate_chunk=A_CHUNK,
                    emit_v=False, n_tiles=batch * seq // ROW_TILE, skewed=True)
    zb = jnp.zeros((n_even, batch * SUBLANES, D_B), F32)
    zc = jnp.zeros((n_odd, batch * SUBLANES, D_C), F32)
    y_p, tb_p, th_p, tc_p, _ = _trunk(x_prompt.reshape(batch * seq, D_MODEL), zb, zb, zc, weights, cfg_p)

    cfg_s = TileCfg(n_seq=ROW_TILE // dec_seq, seq_rows=dec_seq, tiles_per_seq=1, gate_chunk=dec_seq,
                    emit_v=True, n_tiles=dec_batch * dec_seq // ROW_TILE, skewed=False)
    sb = jnp.stack([_pad_state(cache_b_conv[e]) for e in range(n_even)])
    sh = jnp.stack([_pad_state(state_b_h[e][:, None, :]) for e in range(n_even)])
    sc = jnp.stack([_pad_state(cache_c_conv[o]) for o in range(n_odd)])
    y_s, tb_s, th_s, tc_s, v_s = _trunk(x_sample.reshape(dec_batch * dec_seq, D_MODEL), sb, sh, sc, weights,
                                         cfg_s)

    def tails(ts, n, keep):
        t = jnp.stack(ts).reshape(len(ts), n, SUBLANES, -1)
        return t[:, :, SUBLANES - keep:, :]

    return (y_p.reshape(batch, seq, D_MODEL),
            y_s.reshape(dec_batch, dec_seq, D_MODEL),
            jnp.stack(v_s).reshape(n_even, dec_batch, dec_seq, D_A),
            tails(tb_p, batch, B_CONV - 1),
            tails(th_p, batch, 1)[:, :, 0, :],
            tails(tc_p, batch, C_CONV - 1),
            tails(tb_s, dec_batch, B_CONV - 1),
            tails(th_s, dec_batch, 1)[:, :, 0, :],
            tails(tc_s, dec_batch, C_CONV - 1))
```

```python
import functools
from typing import NamedTuple

import jax
import jax.numpy as jnp
from jax import lax
from jax.experimental import pallas as pl
from jax.experimental.pallas import tpu as pltpu

D_MODEL = 1024
DEPTH = 4
CHUNK = 64
A_CHUNK = 128
D_A = D_MODEL // 2
A_HEADS = 4
A_HEAD_DIM = D_A // A_HEADS
D_B = D_MODEL // 2
B_HEADS = 8
B_HEAD_DIM = D_B // B_HEADS
B_CONV = 4
LRU_C = 8.0
D_C = D_MODEL
C_CONV = 3
D_FF = 4 * D_MODEL
EPS = 1e-6

SUBLANES = 8
V7X_VMEM_BYTES = 64 * 1024 * 1024
VMEM_LIMIT_BYTES = V7X_VMEM_BYTES - 8 * 1024 * 1024
ROW_TILE = 512

F32 = jnp.float32
BF16 = jnp.bfloat16


class TileCfg(NamedTuple):
    n_seq: int
    seq_rows: int
    tiles_per_seq: int
    gate_chunk: int
    emit_v: bool
    n_tiles: int
    skewed: bool


def _rmsnorm(x, g):
    ms = jnp.mean(x * x, axis=-1, keepdims=True)
    return x * lax.rsqrt(ms + EPS) * g


def _shift_rows(x, hist, k, cfg):
    if k == 0:
        return x
    rolled = pltpu.roll(x, k, 0)
    row = lax.broadcasted_iota(jnp.int32, (SUBLANES, x.shape[1]), 0)
    pieces = []
    for s in range(cfg.n_seq):
        lo = s * cfg.seq_rows
        h = pltpu.roll(hist[s * SUBLANES:(s + 1) * SUBLANES], k, 0)
        pieces.append(jnp.where(row < k, h, rolled[lo:lo + SUBLANES]))
        pieces.append(rolled[lo + SUBLANES:lo + cfg.seq_rows])
    return jnp.concatenate(pieces, axis=0)


def _causal_conv(x, hist, w, cfg):
    width = w.shape[0]
    y = w[0:1] * _shift_rows(x, hist, width - 1, cfg)
    for k in range(1, width):
        y = y + w[k:k + 1] * _shift_rows(x, hist, width - 1 - k, cfg)
    return y


def _seq_tails(x, cfg):
    pieces = [x[(s + 1) * cfg.seq_rows - SUBLANES:(s + 1) * cfg.seq_rows] for s in range(cfg.n_seq)]
    return pieces[0] if len(pieces) == 1 else jnp.concatenate(pieces, axis=0)


def _linear_scan(a, b, h_prev, cfg):
    row = lax.broadcasted_iota(jnp.int32, a.shape, 0) & (SUBLANES - 1)
    d = 1
    while d < SUBLANES:
        keep = row >= d
        a_sh = pltpu.roll(a, d, 0)
        b_sh = pltpu.roll(b, d, 0)
        b = jnp.where(keep, a * b_sh + b, b)
        a = jnp.where(keep, a * a_sh, a)
        d *= 2
    groups_per_seq = cfg.seq_rows // SUBLANES
    out = []
    for s in range(cfg.n_seq):
        carry = h_prev[s * SUBLANES + SUBLANES - 1:(s + 1) * SUBLANES]
        for g in range(groups_per_seq):
            lo = s * cfg.seq_rows + g * SUBLANES
            hg = b[lo:lo + SUBLANES] + a[lo:lo + SUBLANES] * carry
            out.append(hg)
            carry = hg[SUBLANES - 1:SUBLANES]
    return jnp.concatenate(out, axis=0)


def _spatial_gating(v, ws_ref, bias_ref, cfg):
    lc = cfg.gate_chunk
    n_chunks = v.shape[0] // lc
    vb = v.astype(BF16)
    pi = lax.shift_right_logical(lax.broadcasted_iota(jnp.int32, (lc, lc), 0), 6)
    pj = lax.shift_right_logical(lax.broadcasted_iota(jnp.int32, (lc, lc), 1), 6)
    mask = pj <= pi
    heads = []
    for h in range(A_HEADS):
        lanes = slice(h * A_HEAD_DIM, (h + 1) * A_HEAD_DIM)
        w = jnp.where(mask, ws_ref[h, 0:lc, 0:lc], 0.0).astype(BF16)
        vh = jnp.concatenate([vb[c * lc:(c + 1) * lc, lanes] for c in range(n_chunks)], axis=1)
        sh = jnp.dot(w, vh, preferred_element_type=F32)
        bias = bias_ref[h, 0:lc, :]
        heads.append(jnp.concatenate(
            [sh[:, c * A_HEAD_DIM:(c + 1) * A_HEAD_DIM] + bias for c in range(n_chunks)], axis=0))
    return jnp.concatenate(heads, axis=1)


def _mlp_up(x, gpre_ref, up_ref):
    hn = _rmsnorm(x, gpre_ref[...]).astype(BF16)
    return jnp.dot(hn, up_ref[...], preferred_element_type=F32)


def _mlp_down(x, hid, down_ref, gpost_ref):
    act = jnp.square(jnp.maximum(hid, 0.0)).astype(BF16)
    f = jnp.dot(act, down_ref[...], preferred_element_type=F32)
    return x + _rmsnorm(f, gpost_ref[...])


def _tile_is_real(cfg):
    return pl.program_id(0) < cfg.n_tiles


def _load_states_at_sequence_start(cfg, pairs, x1_scr):
    i = pl.program_id(0)

    @pl.when(jnp.logical_and(lax.rem(i, cfg.tiles_per_seq) == 0, _tile_is_real(cfg)))
    def _():
        for scr, src in pairs:
            scr[...] = src[...]

    if cfg.skewed:
        @pl.when(i == 0)
        def _():
            x1_scr[...] = jnp.zeros_like(x1_scr)


def _keep_state(cfg, new, old):
    return jnp.where(_tile_is_real(cfg), new, old) if cfg.skewed else new


def _even_layer_kernel(cfg, x_ref, gpre_ref, win_ref, vgain_ref, ws_ref, bias_ref, convw_ref, convb_ref,
                       wgate_ref, ba_ref, bx_ref, lam_ref, wout_ref, gpost_ref, gfpre_ref, up_ref,
                       down_ref, gfpost_ref, hist_in_ref, h_in_ref, *rest):
    rest = list(rest)
    xo_ref, tailx_ref, tailh_ref = rest[:3]
    v_ref = rest[3] if cfg.emit_v else None
    hist_scr, h_scr = rest[-3:-1] if cfg.skewed else rest[-2:]
    x1_scr = rest[-1] if cfg.skewed else None

    _load_states_at_sequence_start(cfg, [(hist_scr, hist_in_ref), (h_scr, h_in_ref)], x1_scr)

    x = x_ref[...]
    hn = _rmsnorm(x, gpre_ref[...]).astype(BF16)
    z = jnp.dot(hn, win_ref[...], preferred_element_type=F32)
    if cfg.skewed:
        x1_prev = x1_scr[...]
        hid_prev = _mlp_up(x1_prev, gfpre_ref, up_ref)

    u = jax.nn.gelu(z[:, 0:D_A])
    v = _rmsnorm(jax.nn.gelu(z[:, D_A:2 * D_A]), vgain_ref[...])
    xb = z[:, 2 * D_A:2 * D_A + D_B]
    gb = z[:, 2 * D_A + D_B:]
    if v_ref is not None:
        v_ref[...] = v
    hist = hist_scr[...]
    xc = _causal_conv(xb, hist, convw_ref[...], cfg) + convb_ref[...]
    new_hist = _keep_state(cfg, _seq_tails(xb, cfg), hist)
    hist_scr[...] = new_hist
    tailx_ref[...] = new_hist

    gate = _spatial_gating(v, ws_ref, bias_ref, cfg)
    half = D_B // 2
    xcb = xc.astype(BF16)
    g0 = jnp.dot(xcb[:, :half], wgate_ref[0], preferred_element_type=F32)
    g1 = jnp.dot(xcb[:, half:], wgate_ref[1], preferred_element_type=F32)
    if cfg.skewed:
        xo_ref[...] = _mlp_down(x1_prev, hid_prev, down_ref, gfpost_ref)

    a_out = u * gate
    r = jax.nn.sigmoid(jnp.concatenate([g0[:, :half], g1[:, :half]], axis=1) + ba_ref[...])
    ig = jax.nn.sigmoid(jnp.concatenate([g0[:, half:], g1[:, half:]], axis=1) + bx_ref[...])
    nl = -lam_ref[...]
    softplus = jnp.maximum(nl, 0.0) + jnp.log1p(jnp.exp(-jnp.abs(nl)))
    log_a = -LRU_C * r * softplus
    a = jnp.exp(log_a)
    bterm = jnp.sqrt(1.0 - a * a) * ig * xc
    h_prev = h_scr[...]
    hseq = _linear_scan(a, bterm, h_prev, cfg)
    new_h = _keep_state(cfg, _seq_tails(hseq, cfg), h_prev)
    h_scr[...] = new_h
    tailh_ref[...] = new_h

    b_out = hseq * jax.nn.gelu(gb)
    mix = jnp.concatenate([a_out, b_out], axis=1).astype(BF16)
    m = jnp.dot(mix, wout_ref[...], preferred_element_type=F32)
    x1 = x + _rmsnorm(m, gpost_ref[...])
    if cfg.skewed:
        x1_scr[...] = x1
    else:
        xo_ref[...] = _mlp_down(x1, _mlp_up(x1, gfpre_ref, up_ref), down_ref, gfpost_ref)


def _odd_layer_kernel(cfg, x_ref, gpre_ref, win_ref, convw_ref, wout_ref, gpost_ref, gfpre_ref, up_ref,
                      down_ref, gfpost_ref, hist_in_ref, xo_ref, tail_ref, hist_scr, *rest):
    x1_scr = rest[0] if cfg.skewed else None
    _load_states_at_sequence_start(cfg, [(hist_scr, hist_in_ref)], x1_scr)

    x = x_ref[...]
    hn = _rmsnorm(x, gpre_ref[...]).astype(BF16)
    z = jnp.dot(hn, win_ref[...], preferred_element_type=F32)
    if cfg.skewed:
        x1_prev = x1_scr[...]
        xo_ref[...] = _mlp_down(x1_prev, _mlp_up(x1_prev, gfpre_ref, up_ref), down_ref, gfpost_ref)

    bg = z[:, 0:D_C]
    p = z[:, D_C:2 * D_C] * z[:, 2 * D_C:]
    hist = hist_scr[...]
    y = _causal_conv(p, hist, convw_ref[...], cfg)
    new_hist = _keep_state(cfg, _seq_tails(p, cfg), hist)
    hist_scr[...] = new_hist
    tail_ref[...] = new_hist
    m = jnp.dot((bg * y).astype(BF16), wout_ref[...], preferred_element_type=F32)
    x1 = x + _rmsnorm(m, gpost_ref[...])
    if cfg.skewed:
        x1_scr[...] = x1
    else:
        xo_ref[...] = _mlp_down(x1, _mlp_up(x1, gfpre_ref, up_ref), down_ref, gfpost_ref)


def _resident(stacked, layer):
    zeros = (0,) * (stacked.ndim - 1)
    return pl.BlockSpec((None,) + stacked.shape[1:], lambda i: (layer,) + zeros, pipeline_mode=pl.Buffered(1))


def _mixer_tile(cfg, i):
    return jnp.minimum(i, cfg.n_tiles - 1)


def _state_spec(cfg, width):
    return pl.BlockSpec((cfg.n_seq * SUBLANES, width),
                        lambda i: (_mixer_tile(cfg, i) // cfg.tiles_per_seq, 0))


def _row_in_spec(cfg, width):
    return pl.BlockSpec((ROW_TILE, width), lambda i: (_mixer_tile(cfg, i), 0))


def _row_out_spec(cfg, width):
    if cfg.skewed:
        return pl.BlockSpec((ROW_TILE, width), lambda i: (jnp.maximum(i - 1, 0), 0))
    return pl.BlockSpec((ROW_TILE, width), lambda i: (i, 0))


def _grid(cfg):
    return (cfg.n_tiles + 1,) if cfg.skewed else (cfg.n_tiles,)


def _compiler_params():
    return pltpu.CompilerParams(dimension_semantics=("arbitrary",), vmem_limit_bytes=VMEM_LIMIT_BYTES)


def _even_layer(cfg, x, hist, h0, w, layer):
    rows = x.shape[0]
    n_state_rows = hist.shape[0]
    e = layer // 2
    consts = [(w["gpre"], layer), (w["win"], e), (w["vgain"], e), (w["ws"], e), (w["bias"], e),
              (w["convw"], e), (w["convb"], e), (w["wgate"], e), (w["ba"], e), (w["bx"], e), (w["lam"], e),
              (w["wout"], e), (w["gpost"], layer), (w["gfpre"], layer), (w["up"], layer), (w["down"], layer),
              (w["gfpost"], layer)]
    in_specs = ([_row_in_spec(cfg, D_MODEL)] + [_resident(c, k) for c, k in consts]
                + [_state_spec(cfg, D_B), _state_spec(cfg, D_B)])
    out_shape = [jax.ShapeDtypeStruct((rows, D_MODEL), F32),
                 jax.ShapeDtypeStruct((n_state_rows, D_B), F32),
                 jax.ShapeDtypeStruct((n_state_rows, D_B), F32)]
    out_specs = [_row_out_spec(cfg, D_MODEL), _state_spec(cfg, D_B), _state_spec(cfg, D_B)]
    if cfg.emit_v:
        out_shape.append(jax.ShapeDtypeStruct((rows, D_A), F32))
        out_specs.append(_row_in_spec(cfg, D_A))
    scratch = [pltpu.VMEM((cfg.n_seq * SUBLANES, D_B), F32), pltpu.VMEM((cfg.n_seq * SUBLANES, D_B), F32)]
    if cfg.skewed:
        scratch.append(pltpu.VMEM((ROW_TILE, D_MODEL), F32))
    return pl.pallas_call(
        functools.partial(_even_layer_kernel, cfg),
        grid=_grid(cfg),
        in_specs=in_specs,
        out_specs=out_specs,
        out_shape=out_shape,
        scratch_shapes=scratch,
        compiler_params=_compiler_params(),
        name="even_layer",
    )(x, *[c for c, _ in consts], hist, h0)


def _odd_layer(cfg, x, hist, w, layer):
    rows = x.shape[0]
    o = layer // 2
    consts = [(w["gpre"], layer), (w["cwin"], o), (w["cconvw"], o), (w["cwout"], o), (w["gpost"], layer),
              (w["gfpre"], layer), (w["up"], layer), (w["down"], layer), (w["gfpost"], layer)]
    in_specs = ([_row_in_spec(cfg, D_MODEL)] + [_resident(c, k) for c, k in consts]
                + [_state_spec(cfg, D_C)])
    out_shape = [jax.ShapeDtypeStruct((rows, D_MODEL), F32),
                 jax.ShapeDtypeStruct((hist.shape[0], D_C), F32)]
    out_specs = [_row_out_spec(cfg, D_MODEL), _state_spec(cfg, D_C)]
    scratch = [pltpu.VMEM((cfg.n_seq * SUBLANES, D_C), F32)]
    if cfg.skewed:
        scratch.append(pltpu.VMEM((ROW_TILE, D_MODEL), F32))
    return pl.pallas_call(
        functools.partial(_odd_layer_kernel, cfg),
        grid=_grid(cfg),
        in_specs=in_specs,
        out_specs=out_specs,
        out_shape=out_shape,
        scratch_shapes=scratch,
        compiler_params=_compiler_params(),
        name="odd_layer",
    )(x, *[c for c, _ in consts], hist)


def _block_diag_gates(wa, wx):
    heads_per_half = B_HEADS // 2
    eye = jnp.eye(heads_per_half, dtype=wa.dtype)

    def bd(w):
        return jnp.einsum("eqhij,hg->eqhigj", w, eye).reshape(w.shape[0], 2, D_B // 2, D_B // 2)

    split = lambda w: w.reshape(w.shape[0], 2, heads_per_half, B_HEAD_DIM, B_HEAD_DIM)
    return jnp.concatenate([bd(split(wa)), bd(split(wx))], axis=-1).astype(BF16)


def _pad_state(rows):
    n, r, c = rows.shape
    return jnp.pad(rows, ((0, 0), (SUBLANES - r, 0), (0, 0))).reshape(n * SUBLANES, c)


def _stacked_weights(p):
    row = lambda v: v[:, None, :]
    a_b_s = p["a_b_s"]
    return dict(
        gpre=row(p["norm_mix_pre"]), gpost=row(p["norm_mix_post"]), gfpre=row(p["norm_ffn_pre"]),
        gfpost=row(p["norm_ffn_post"]), up=p["mlp_up"].astype(BF16), down=p["mlp_down"].astype(BF16),
        win=p["w_in_even"].astype(BF16), vgain=row(p["a_v_gain"]), ws=p["a_w_s"],
        bias=jnp.broadcast_to(a_b_s[..., None], a_b_s.shape + (A_HEAD_DIM,)), convw=p["b_conv_w"],
        convb=row(p["b_conv_b"]), wgate=_block_diag_gates(p["b_wa"], p["b_wx"]), ba=row(p["b_ba"]),
        bx=row(p["b_bx"]), lam=row(p["b_lambda"]), wout=p["w_out_even"].astype(BF16),
        cwin=p["c_w_in"].astype(BF16), cconvw=p["c_conv_w"], cwout=p["c_w_out"].astype(BF16))


def _trunk(x, b_conv, b_h, c_conv, weights, cfg):
    tails_b, tails_h, tails_c, v_rows = [], [], [], []
    for l in range(DEPTH):
        if l % 2 == 0:
            outs = _even_layer(cfg, x, b_conv[l // 2], b_h[l // 2], weights, l)
            x = outs[0]
            tails_b.append(outs[1])
            tails_h.append(outs[2])
            if cfg.emit_v:
                v_rows.append(outs[3])
        else:
            x, tail = _odd_layer(cfg, x, c_conv[l // 2], weights, l)
            tails_c.append(tail)
    return x, tails_b, tails_h, tails_c, v_rows


def kernel(x_prompt, x_sample, cache_b_conv, state_b_h, cache_c_conv, norm_mix_pre, norm_mix_post, norm_ffn_pre, norm_ffn_post, w_in_even, a_v_gain, a_w_s, a_b_s, b_conv_w, b_conv_b, b_wa, b_ba, b_wx, b_bx, b_lambda, w_out_even, c_w_in, c_conv_w, c_w_out, mlp_up, mlp_down):
    batch, seq, _ = x_prompt.shape
    dec_batch, dec_seq, _ = x_sample.shape
    n_even, n_odd = w_in_even.shape[0], c_w_in.shape[0]
    p = dict(norm_mix_pre=norm_mix_pre, norm_mix_post=norm_mix_post, norm_ffn_pre=norm_ffn_pre,
             norm_ffn_post=norm_ffn_post, w_in_even=w_in_even, a_v_gain=a_v_gain, a_w_s=a_w_s, a_b_s=a_b_s,
             b_conv_w=b_conv_w, b_conv_b=b_conv_b, b_wa=b_wa, b_wx=b_wx, b_ba=b_ba, b_bx=b_bx,
             b_lambda=b_lambda, w_out_even=w_out_even, c_w_in=c_w_in, c_conv_w=c_conv_w, c_w_out=c_w_out,
             mlp_up=mlp_up, mlp_down=mlp_down)
    weights = _stacked_weights(p)

    cfg_p = TileCfg(n_seq=1, seq_rows=ROW_TILE, tiles_per_seq=seq // ROW_TILE, gate_chunk=A_CHUNK,
                    emit_v=False, n_tiles=batch * seq // ROW_TILE, skewed=True)
    zb = jnp.zeros((n_even, batch * SUBLANES, D_B), F32)
    zc = jnp.zeros((n_odd, batch * SUBLANES, D_C), F32)
    y_p, tb_p, th_p, tc_p, _ = _trunk(x_prompt.reshape(batch * seq, D_MODEL), zb, zb, zc, weights, cfg_p)

    cfg_s = TileCfg(n_seq=ROW_TILE // dec_seq, seq_rows=dec_seq, tiles_per_seq=1, gate_chunk=dec_seq,
                    emit_v=True, n_tiles=dec_batch * dec_seq // ROW_TILE, skewed=False)
    sb = jnp.stack([_pad_state(cache_b_conv[e]) for e in range(n_even)])
    sh = jnp.stack([_pad_state(state_b_h[e][:, None, :]) for e in range(n_even)])
    sc = jnp.stack([_pad_state(cache_c_conv[o]) for o in range(n_odd)])
    y_s, tb_s, th_s, tc_s, v_s = _trunk(x_sample.reshape(dec_batch * dec_seq, D_MODEL), sb, sh, sc, weights,
                                         cfg_s)

    def tails(ts, n, keep):
        t = jnp.stack(ts).reshape(len(ts), n, SUBLANES, -1)
        return t[:, :, SUBLANES - keep:, :]

    return (y_p.reshape(batch, seq, D_MODEL),
            y_s.reshape(dec_batch, dec_seq, D_MODEL),
            jnp.stack(v_s).reshape(n_even, dec_batch, dec_seq, D_A),
            tails(tb_p, batch, B_CONV - 1),
            tails(th_p, batch, 1)[:, :, 0, :],
            tails(tc_p, batch, C_CONV - 1),
            tails(tb_s, dec_batch, B_CONV - 1),
            tails(th_s, dec_batch, 1)[:, :, 0, :],
            tails(tc_s, dec_batch, C_CONV - 1))
```

```python
import functools
from typing import NamedTuple

import jax
import jax.numpy as jnp
from jax import lax
from jax.experimental import pallas as pl
from jax.experimental.pallas import tpu as pltpu

D_MODEL = 1024
DEPTH = 4
CHUNK = 64
A_CHUNK = 128
D_A = D_MODEL // 2
A_HEADS = 4
A_HEAD_DIM = D_A // A_HEADS
D_B = D_MODEL // 2
B_HEADS = 8
B_HEAD_DIM = D_B // B_HEADS
B_CONV = 4
LRU_C = 8.0
D_C = D_MODEL
C_CONV = 3
D_FF = 4 * D_MODEL
EPS = 1e-6

SUBLANES = 8
V7X_VMEM_BYTES = 64 * 1024 * 1024
VMEM_LIMIT_BYTES = V7X_VMEM_BYTES - 8 * 1024 * 1024
ROW_TILE = 512
TILES_PER_STEP = 2
STEP_ROWS = ROW_TILE * TILES_PER_STEP

F32 = jnp.float32
BF16 = jnp.bfloat16


class TileCfg(NamedTuple):
    n_seq: int
    seq_rows: int
    steps_per_seq: int
    gate_chunk: int
    emit_v: bool
    n_steps: int
    skewed: bool

    @property
    def state_rows(self):
        groups = TILES_PER_STEP if self.n_seq > 1 else 1
        return groups * self.n_seq * SUBLANES

    def tile_state(self, t):
        if self.n_seq == 1:
            return slice(0, SUBLANES)
        return slice(t * self.n_seq * SUBLANES, (t + 1) * self.n_seq * SUBLANES)


def _rmsnorm(x, g):
    ms = jnp.mean(x * x, axis=-1, keepdims=True)
    return x * lax.rsqrt(ms + EPS) * g


def _shift_rows(x, hist, k, cfg):
    if k == 0:
        return x
    rolled = pltpu.roll(x, k, 0)
    row = lax.broadcasted_iota(jnp.int32, (SUBLANES, x.shape[1]), 0)
    pieces = []
    for s in range(cfg.n_seq):
        lo = s * cfg.seq_rows
        h = pltpu.roll(hist[s * SUBLANES:(s + 1) * SUBLANES], k, 0)
        pieces.append(jnp.where(row < k, h, rolled[lo:lo + SUBLANES]))
        pieces.append(rolled[lo + SUBLANES:lo + cfg.seq_rows])
    return jnp.concatenate(pieces, axis=0)


def _causal_conv(x, hist, w, cfg):
    width = w.shape[0]
    y = w[0:1] * _shift_rows(x, hist, width - 1, cfg)
    for k in range(1, width):
        y = y + w[k:k + 1] * _shift_rows(x, hist, width - 1 - k, cfg)
    return y


def _seq_tails(x, cfg):
    pieces = [x[(s + 1) * cfg.seq_rows - SUBLANES:(s + 1) * cfg.seq_rows] for s in range(cfg.n_seq)]
    return pieces[0] if len(pieces) == 1 else jnp.concatenate(pieces, axis=0)


def _linear_scan(a, b, h_prev, cfg):
    row = lax.broadcasted_iota(jnp.int32, a.shape, 0) & (SUBLANES - 1)
    d = 1
    while d < SUBLANES:
        keep = row >= d
        a_sh = pltpu.roll(a, d, 0)
        b_sh = pltpu.roll(b, d, 0)
        b = jnp.where(keep, a * b_sh + b, b)
        a = jnp.where(keep, a * a_sh, a)
        d *= 2
    groups_per_seq = cfg.seq_rows // SUBLANES
    out = []
    for s in range(cfg.n_seq):
        carry = h_prev[s * SUBLANES + SUBLANES - 1:(s + 1) * SUBLANES]
        for g in range(groups_per_seq):
            lo = s * cfg.seq_rows + g * SUBLANES
            hg = b[lo:lo + SUBLANES] + a[lo:lo + SUBLANES] * carry
            out.append(hg)
            carry = hg[SUBLANES - 1:SUBLANES]
    return jnp.concatenate(out, axis=0)


def _spatial_gating(v, ws_ref, bias_ref, cfg):
    lc = cfg.gate_chunk
    n_chunks = v.shape[0] // lc
    vb = v.astype(BF16)
    pi = lax.shift_right_logical(lax.broadcasted_iota(jnp.int32, (lc, lc), 0), 6)
    pj = lax.shift_right_logical(lax.broadcasted_iota(jnp.int32, (lc, lc), 1), 6)
    mask = pj <= pi
    heads = []
    for h in range(A_HEADS):
        lanes = slice(h * A_HEAD_DIM, (h + 1) * A_HEAD_DIM)
        w = jnp.where(mask, ws_ref[h, 0:lc, 0:lc], 0.0).astype(BF16)
        vh = jnp.concatenate([vb[c * lc:(c + 1) * lc, lanes] for c in range(n_chunks)], axis=1)
        sh = jnp.dot(w, vh, preferred_element_type=F32)
        bias = bias_ref[h, 0:lc, :]
        heads.append(jnp.concatenate(
            [sh[:, c * A_HEAD_DIM:(c + 1) * A_HEAD_DIM] + bias for c in range(n_chunks)], axis=0))
    return jnp.concatenate(heads, axis=1)


def _mixer_in(x, gpre_ref, win_ref):
    hn = _rmsnorm(x, gpre_ref[...]).astype(BF16)
    return jnp.dot(hn, win_ref[...], preferred_element_type=F32)


def _mlp_up(x, gpre_ref, up_ref):
    hn = _rmsnorm(x, gpre_ref[...]).astype(BF16)
    return jnp.dot(hn, up_ref[...], preferred_element_type=F32)


def _mlp_down(x, hid, down_ref, gpost_ref):
    act = jnp.square(jnp.maximum(hid, 0.0)).astype(BF16)
    f = jnp.dot(act, down_ref[...], preferred_element_type=F32)
    return x + _rmsnorm(f, gpost_ref[...])


def _even_mixer(cfg, x, z, refs, states, tile, between):
    (vgain_ref, ws_ref, bias_ref, convw_ref, convb_ref, wgate_ref, ba_ref, bx_ref, lam_ref, wout_ref,
     gpost_ref) = refs
    hist_scr, h_scr, tailx_ref, tailh_ref, v_ref = states
    srows = cfg.tile_state(tile)
    u = jax.nn.gelu(z[:, 0:D_A])
    v = _rmsnorm(jax.nn.gelu(z[:, D_A:2 * D_A]), vgain_ref[...])
    xb = z[:, 2 * D_A:2 * D_A + D_B]
    gb = z[:, 2 * D_A + D_B:]
    if v_ref is not None:
        v_ref[tile * ROW_TILE:(tile + 1) * ROW_TILE, :] = v
    xc = _causal_conv(xb, hist_scr[srows, :], convw_ref[...], cfg) + convb_ref[...]
    new_hist = _seq_tails(xb, cfg)
    hist_scr[srows, :] = new_hist
    tailx_ref[srows, :] = new_hist

    gate = _spatial_gating(v, ws_ref, bias_ref, cfg)
    half = D_B // 2
    xcb = xc.astype(BF16)
    g0 = jnp.dot(xcb[:, :half], wgate_ref[0], preferred_element_type=F32)
    g1 = jnp.dot(xcb[:, half:], wgate_ref[1], preferred_element_type=F32)
    between()

    a_out = u * gate
    r = jax.nn.sigmoid(jnp.concatenate([g0[:, :half], g1[:, :half]], axis=1) + ba_ref[...])
    ig = jax.nn.sigmoid(jnp.concatenate([g0[:, half:], g1[:, half:]], axis=1) + bx_ref[...])
    nl = -lam_ref[...]
    softplus = jnp.maximum(nl, 0.0) + jnp.log1p(jnp.exp(-jnp.abs(nl)))
    log_a = -LRU_C * r * softplus
    a = jnp.exp(log_a)
    bterm = jnp.sqrt(1.0 - a * a) * ig * xc
    hseq = _linear_scan(a, bterm, h_scr[srows, :], cfg)
    new_h = _seq_tails(hseq, cfg)
    h_scr[srows, :] = new_h
    tailh_ref[srows, :] = new_h

    b_out = hseq * jax.nn.gelu(gb)
    mix = jnp.concatenate([a_out, b_out], axis=1).astype(BF16)
    m = jnp.dot(mix, wout_ref[...], preferred_element_type=F32)
    return x + _rmsnorm(m, gpost_ref[...])


def _odd_mixer(cfg, x, z, refs, states, tile, between):
    convw_ref, wout_ref, gpost_ref = refs
    hist_scr, tail_ref = states
    srows = cfg.tile_state(tile)
    between()
    bg = z[:, 0:D_C]
    p = z[:, D_C:2 * D_C] * z[:, 2 * D_C:]
    y = _causal_conv(p, hist_scr[srows, :], convw_ref[...], cfg)
    new_hist = _seq_tails(p, cfg)
    hist_scr[srows, :] = new_hist
    tail_ref[srows, :] = new_hist
    m = jnp.dot((bg * y).astype(BF16), wout_ref[...], preferred_element_type=F32)
    return x + _rmsnorm(m, gpost_ref[...])


def _paired(mixer, x, z_of, mlp_refs, mlp_x, write_mlp):
    gfpre_ref, up_ref, down_ref, gfpost_ref = mlp_refs
    z = z_of(x) if x is not None else None
    hid = _mlp_up(mlp_x, gfpre_ref, up_ref) if mlp_x is not None else None

    def mlp_down():
        if mlp_x is not None:
            write_mlp(_mlp_down(mlp_x, hid, down_ref, gfpost_ref))

    if x is None:
        mlp_down()
        return None
    return mixer(x, z, mlp_down)


def _layer_body(cfg, mixer, z_of, mlp_refs, x_ref, xo_ref, x1_scr):
    rows = lambda t: slice(t * ROW_TILE, (t + 1) * ROW_TILE)

    def write_out(t):
        def write(val):
            xo_ref[rows(t), :] = val
        return write

    if not cfg.skewed:
        x1_prev = None
        for t in range(TILES_PER_STEP + 1):
            x = x_ref[rows(t), :] if t < TILES_PER_STEP else None
            x1 = _paired(functools.partial(mixer, t), x, z_of, mlp_refs, x1_prev,
                         write_out(t - 1) if t > 0 else None)
            x1_prev = x1
        return

    i = pl.program_id(0)

    def step(do_mixer, do_mlp):
        for t in range(TILES_PER_STEP):
            x = x_ref[rows(t), :] if do_mixer else None
            mlp_x = x1_scr[rows(t), :] if do_mlp else None
            x1 = _paired(functools.partial(mixer, t), x, z_of, mlp_refs, mlp_x, write_out(t))
            if do_mixer:
                x1_scr[rows(t), :] = x1

    pl.when(i == 0)(lambda: step(True, False))
    pl.when(jnp.logical_and(i > 0, i < cfg.n_steps))(lambda: step(True, True))
    pl.when(i == cfg.n_steps)(lambda: step(False, True))


def _load_states_at_sequence_start(cfg, pairs):
    i = pl.program_id(0)

    @pl.when(jnp.logical_and(lax.rem(i, cfg.steps_per_seq) == 0, i < cfg.n_steps))
    def _():
        for scr, src in pairs:
            scr[...] = src[...]


def _even_layer_kernel(cfg, x_ref, gpre_ref, win_ref, vgain_ref, ws_ref, bias_ref, convw_ref,
                       convb_ref, wgate_ref, ba_ref, bx_ref, lam_ref, wout_ref, gpost_ref, gfpre_ref, up_ref,
                       down_ref, gfpost_ref, hist_in_ref, h_in_ref, *rest):
    rest = list(rest)
    xo_ref, tailx_ref, tailh_ref = rest[:3]
    n_out = 4 if cfg.emit_v else 3
    v_ref = rest[3] if cfg.emit_v else None
    hist_scr, h_scr = rest[n_out:n_out + 2]
    x1_scr = rest[n_out + 2] if cfg.skewed else None
    mixer_refs = (vgain_ref, ws_ref, bias_ref, convw_ref, convb_ref, wgate_ref, ba_ref, bx_ref, lam_ref,
                  wout_ref, gpost_ref)
    states = (hist_scr, h_scr, tailx_ref, tailh_ref, v_ref)
    _load_states_at_sequence_start(cfg, [(hist_scr, hist_in_ref), (h_scr, h_in_ref)])

    def mixer(tile, x, z, between):
        return _even_mixer(cfg, x, z, mixer_refs, states, tile, between)

    _layer_body(cfg, mixer, lambda x: _mixer_in(x, gpre_ref, win_ref), (gfpre_ref, up_ref, down_ref, gfpost_ref),
                x_ref, xo_ref, x1_scr)


def _odd_layer_kernel(cfg, x_ref, gpre_ref, win_ref, convw_ref, wout_ref, gpost_ref, gfpre_ref,
                      up_ref, down_ref, gfpost_ref, hist_in_ref, xo_ref, tail_ref, hist_scr, *rest):
    x1_scr = rest[0] if cfg.skewed else None
    _load_states_at_sequence_start(cfg, [(hist_scr, hist_in_ref)])

    def mixer(tile, x, z, between):
        return _odd_mixer(cfg, x, z, (convw_ref, wout_ref, gpost_ref), (hist_scr, tail_ref), tile, between)

    _layer_body(cfg, mixer, lambda x: _mixer_in(x, gpre_ref, win_ref), (gfpre_ref, up_ref, down_ref, gfpost_ref),
                x_ref, xo_ref, x1_scr)


def _resident(stacked, layer):
    zeros = (0,) * (stacked.ndim - 1)
    return pl.BlockSpec((None,) + stacked.shape[1:], lambda i: (layer,) + zeros, pipeline_mode=pl.Buffered(1))


def _mixer_step(cfg, i):
    return jnp.minimum(i, cfg.n_steps - 1)


def _state_spec(cfg, width):
    return pl.BlockSpec((cfg.state_rows, width), lambda i: (_mixer_step(cfg, i) // cfg.steps_per_seq, 0))


def _row_in_spec(cfg, width):
    return pl.BlockSpec((STEP_ROWS, width), lambda i: (_mixer_step(cfg, i), 0))


def _row_out_spec(cfg, width):
    if cfg.skewed:
        return pl.BlockSpec((STEP_ROWS, width), lambda i: (jnp.maximum(i - 1, 0), 0))
    return pl.BlockSpec((STEP_ROWS, width), lambda i: (i, 0))


def _grid(cfg):
    return (cfg.n_steps + 1,) if cfg.skewed else (cfg.n_steps,)


def _skew_scratch(cfg):
    return [pltpu.VMEM((STEP_ROWS, D_MODEL), F32)] if cfg.skewed else []


def _compiler_params():
    return pltpu.CompilerParams(dimension_semantics=("arbitrary",), vmem_limit_bytes=VMEM_LIMIT_BYTES)


def _even_layer(cfg, x, hist, h0, w, layer):
    rows = x.shape[0]
    n_state_rows = hist.shape[0]
    e = layer // 2
    consts = [(w["gpre"], layer), (w["win"], e), (w["vgain"], e), (w["ws"], e), (w["bias"], e),
              (w["convw"], e), (w["convb"], e), (w["wgate"], e), (w["ba"], e), (w["bx"], e), (w["lam"], e),
              (w["wout"], e), (w["gpost"], layer), (w["gfpre"], layer), (w["up"], layer), (w["down"], layer),
              (w["gfpost"], layer)]
    in_specs = ([_row_in_spec(cfg, D_MODEL)] + [_resident(c, k) for c, k in consts]
                + [_state_spec(cfg, D_B), _state_spec(cfg, D_B)])
    out_shape = [jax.ShapeDtypeStruct((rows, D_MODEL), F32),
                 jax.ShapeDtypeStruct((n_state_rows, D_B), F32),
                 jax.ShapeDtypeStruct((n_state_rows, D_B), F32)]
    out_specs = [_row_out_spec(cfg, D_MODEL), _state_spec(cfg, D_B), _state_spec(cfg, D_B)]
    if cfg.emit_v:
        out_shape.append(jax.ShapeDtypeStruct((rows, D_A), F32))
        out_specs.append(_row_in_spec(cfg, D_A))
    scratch = ([pltpu.VMEM((cfg.state_rows, D_B), F32), pltpu.VMEM((cfg.state_rows, D_B), F32)]
               + _skew_scratch(cfg))
    return pl.pallas_call(
        functools.partial(_even_layer_kernel, cfg),
        grid=_grid(cfg),
        in_specs=in_specs,
        out_specs=out_specs,
        out_shape=out_shape,
        scratch_shapes=scratch,
        compiler_params=_compiler_params(),
        name="even_layer",
    )(x, *[c for c, _ in consts], hist, h0)


def _odd_layer(cfg, x, hist, w, layer):
    rows = x.shape[0]
    o = layer // 2
    consts = [(w["gpre"], layer), (w["cwin"], o), (w["cconvw"], o), (w["cwout"], o), (w["gpost"], layer),
              (w["gfpre"], layer), (w["up"], layer), (w["down"], layer), (w["gfpost"], layer)]
    in_specs = ([_row_in_spec(cfg, D_MODEL)] + [_resident(c, k) for c, k in consts]
                + [_state_spec(cfg, D_C)])
    out_shape = [jax.ShapeDtypeStruct((rows, D_MODEL), F32),
                 jax.ShapeDtypeStruct((hist.shape[0], D_C), F32)]
    out_specs = [_row_out_spec(cfg, D_MODEL), _state_spec(cfg, D_C)]
    scratch = [pltpu.VMEM((cfg.state_rows, D_C), F32)] + _skew_scratch(cfg)
    return pl.pallas_call(
        functools.partial(_odd_layer_kernel, cfg),
        grid=_grid(cfg),
        in_specs=in_specs,
        out_specs=out_specs,
        out_shape=out_shape,
        scratch_shapes=scratch,
        compiler_params=_compiler_params(),
        name="odd_layer",
    )(x, *[c for c, _ in consts], hist)


def _block_diag_gates(wa, wx):
    heads_per_half = B_HEADS // 2
    eye = jnp.eye(heads_per_half, dtype=wa.dtype)

    def bd(w):
        return jnp.einsum("eqhij,hg->eqhigj", w, eye).reshape(w.shape[0], 2, D_B // 2, D_B // 2)

    split = lambda w: w.reshape(w.shape[0], 2, heads_per_half, B_HEAD_DIM, B_HEAD_DIM)
    return jnp.concatenate([bd(split(wa)), bd(split(wx))], axis=-1).astype(BF16)


def _pad_state(rows):
    n, r, c = rows.shape
    return jnp.pad(rows, ((0, 0), (SUBLANES - r, 0), (0, 0))).reshape(n * SUBLANES, c)


def _stacked_weights(p):
    row = lambda v: v[:, None, :]
    a_b_s = p["a_b_s"]
    return dict(
        gpre=row(p["norm_mix_pre"]), gpost=row(p["norm_mix_post"]), gfpre=row(p["norm_ffn_pre"]),
        gfpost=row(p["norm_ffn_post"]), up=p["mlp_up"].astype(BF16), down=p["mlp_down"].astype(BF16),
        win=p["w_in_even"].astype(BF16), vgain=row(p["a_v_gain"]), ws=p["a_w_s"],
        bias=jnp.broadcast_to(a_b_s[..., None], a_b_s.shape + (A_HEAD_DIM,)), convw=p["b_conv_w"],
        convb=row(p["b_conv_b"]), wgate=_block_diag_gates(p["b_wa"], p["b_wx"]), ba=row(p["b_ba"]),
        bx=row(p["b_bx"]), lam=row(p["b_lambda"]), wout=p["w_out_even"].astype(BF16),
        cwin=p["c_w_in"].astype(BF16), cconvw=p["c_conv_w"], cwout=p["c_w_out"].astype(BF16))


def _trunk(x, b_conv, b_h, c_conv, weights, cfg):
    tails_b, tails_h, tails_c, v_rows = [], [], [], []
    for l in range(DEPTH):
        if l % 2 == 0:
            outs = _even_layer(cfg, x, b_conv[l // 2], b_h[l // 2], weights, l)
            x = outs[0]
            tails_b.append(outs[1])
            tails_h.append(outs[2])
            if cfg.emit_v:
                v_rows.append(outs[3])
        else:
            x, tail = _odd_layer(cfg, x, c_conv[l // 2], weights, l)
            tails_c.append(tail)
    return x, tails_b, tails_h, tails_c, v_rows


def kernel(x_prompt, x_sample, cache_b_conv, state_b_h, cache_c_conv, norm_mix_pre, norm_mix_post, norm_ffn_pre, norm_ffn_post, w_in_even, a_v_gain, a_w_s, a_b_s, b_conv_w, b_conv_b, b_wa, b_ba, b_wx, b_bx, b_lambda, w_out_even, c_w_in, c_conv_w, c_w_out, mlp_up, mlp_down):
    batch, seq, _ = x_prompt.shape
    dec_batch, dec_seq, _ = x_sample.shape
    n_even, n_odd = w_in_even.shape[0], c_w_in.shape[0]
    p = dict(norm_mix_pre=norm_mix_pre, norm_mix_post=norm_mix_post, norm_ffn_pre=norm_ffn_pre,
             norm_ffn_post=norm_ffn_post, w_in_even=w_in_even, a_v_gain=a_v_gain, a_w_s=a_w_s, a_b_s=a_b_s,
             b_conv_w=b_conv_w, b_conv_b=b_conv_b, b_wa=b_wa, b_wx=b_wx, b_ba=b_ba, b_bx=b_bx,
             b_lambda=b_lambda, w_out_even=w_out_even, c_w_in=c_w_in, c_conv_w=c_conv_w, c_w_out=c_w_out,
             mlp_up=mlp_up, mlp_down=mlp_down)
    weights = _stacked_weights(p)

    cfg_p = TileCfg(n_seq=1, seq_rows=ROW_TILE, steps_per_seq=seq // STEP_ROWS, gate_chunk=A_CHUNK,
                    emit_v=False, n_steps=batch * seq // STEP_ROWS, skewed=True)
    zb = jnp.zeros((n_even, batch * SUBLANES, D_B), F32)
    zc = jnp.zeros((n_odd, batch * SUBLANES, D_C), F32)
    y_p, tb_p, th_p, tc_p, _ = _trunk(x_prompt.reshape(batch * seq, D_MODEL), zb, zb, zc, weights, cfg_p)

    cfg_s = TileCfg(n_seq=ROW_TILE // dec_seq, seq_rows=dec_seq, steps_per_seq=1, gate_chunk=dec_seq,
                    emit_v=True, n_steps=dec_batch * dec_seq // STEP_ROWS, skewed=False)
    sb = jnp.stack([_pad_state(cache_b_conv[e]) for e in range(n_even)])
    sh = jnp.stack([_pad_state(state_b_h[e][:, None, :]) for e in range(n_even)])
    sc = jnp.stack([_pad_state(cache_c_conv[o]) for o in range(n_odd)])
    y_s, tb_s, th_s, tc_s, v_s = _trunk(x_sample.reshape(dec_batch * dec_seq, D_MODEL), sb, sh, sc, weights,
                                         cfg_s)

    def tails(ts, n, keep):
        t = jnp.stack(ts).reshape(len(ts), n, SUBLANES, -1)
        return t[:, :, SUBLANES - keep:, :]

    return (y_p.reshape(batch, seq, D_MODEL),
            y_s.reshape(dec_batch, dec_seq, D_MODEL),
            jnp.stack(v_s).reshape(n_even, dec_batch, dec_seq, D_A),
            tails(tb_p, batch, B_CONV - 1),
            tails(th_p, batch, 1)[:, :, 0, :],
            tails(tc_p, batch, C_CONV - 1),
            tails(tb_s, dec_batch, B_CONV - 1),
            tails(th_s, dec_batch, 1)[:, :, 0, :],
            tails(tc_s, dec_batch, C_CONV - 1))
```

```python
import functools
from typing import NamedTuple

import jax
import jax.numpy as jnp
from jax import lax
from jax.experimental import pallas as pl
from jax.experimental.pallas import tpu as pltpu

D_MODEL = 1024
DEPTH = 4
CHUNK = 64
A_CHUNK = 128
D_A = D_MODEL // 2
A_HEADS = 4
A_HEAD_DIM = D_A // A_HEADS
D_B = D_MODEL // 2
B_HEADS = 8
B_HEAD_DIM = D_B // B_HEADS
B_CONV = 4
LRU_C = 8.0
D_C = D_MODEL
C_CONV = 3
D_FF = 4 * D_MODEL
EPS = 1e-6

SUBLANES = 8
V7X_VMEM_BYTES = 64 * 1024 * 1024
VMEM_LIMIT_BYTES = V7X_VMEM_BYTES - 8 * 1024 * 1024
ROW_TILE = 512
TILES_PER_STEP = 1
STEP_ROWS = ROW_TILE * TILES_PER_STEP

F32 = jnp.float32
BF16 = jnp.bfloat16


class TileCfg(NamedTuple):
    n_seq: int
    seq_rows: int
    steps_per_seq: int
    gate_chunk: int
    emit_v: bool
    n_steps: int
    skewed: bool

    @property
    def state_rows(self):
        groups = TILES_PER_STEP if self.n_seq > 1 else 1
        return groups * self.n_seq * SUBLANES

    def tile_state(self, t):
        if self.n_seq == 1:
            return slice(0, SUBLANES)
        return slice(t * self.n_seq * SUBLANES, (t + 1) * self.n_seq * SUBLANES)


def _rmsnorm(x, g):
    ms = jnp.mean(x * x, axis=-1, keepdims=True)
    return x * lax.rsqrt(ms + EPS) * g


def _shift_rows(x, hist, k, cfg):
    if k == 0:
        return x
    rolled = pltpu.roll(x, k, 0)
    row = lax.broadcasted_iota(jnp.int32, (SUBLANES, x.shape[1]), 0)
    pieces = []
    for s in range(cfg.n_seq):
        lo = s * cfg.seq_rows
        h = pltpu.roll(hist[s * SUBLANES:(s + 1) * SUBLANES], k, 0)
        pieces.append(jnp.where(row < k, h, rolled[lo:lo + SUBLANES]))
        pieces.append(rolled[lo + SUBLANES:lo + cfg.seq_rows])
    return jnp.concatenate(pieces, axis=0)


def _causal_conv(x, hist, w, cfg):
    width = w.shape[0]
    y = w[0:1] * _shift_rows(x, hist, width - 1, cfg)
    for k in range(1, width):
        y = y + w[k:k + 1] * _shift_rows(x, hist, width - 1 - k, cfg)
    return y


def _seq_tails(x, cfg):
    pieces = [x[(s + 1) * cfg.seq_rows - SUBLANES:(s + 1) * cfg.seq_rows] for s in range(cfg.n_seq)]
    return pieces[0] if len(pieces) == 1 else jnp.concatenate(pieces, axis=0)


def _linear_scan(a, b, h_prev, cfg):
    row = lax.broadcasted_iota(jnp.int32, a.shape, 0) & (SUBLANES - 1)
    d = 1
    while d < SUBLANES:
        keep = row >= d
        a_sh = pltpu.roll(a, d, 0)
        b_sh = pltpu.roll(b, d, 0)
        b = jnp.where(keep, a * b_sh + b, b)
        a = jnp.where(keep, a * a_sh, a)
        d *= 2
    groups_per_seq = cfg.seq_rows // SUBLANES
    out = []
    for s in range(cfg.n_seq):
        carry = h_prev[s * SUBLANES + SUBLANES - 1:(s + 1) * SUBLANES]
        for g in range(groups_per_seq):
            lo = s * cfg.seq_rows + g * SUBLANES
            hg = b[lo:lo + SUBLANES] + a[lo:lo + SUBLANES] * carry
            out.append(hg)
            carry = hg[SUBLANES - 1:SUBLANES]
    return jnp.concatenate(out, axis=0)


def _spatial_gating(v, ws_ref, bias_ref, cfg):
    lc = cfg.gate_chunk
    n_chunks = v.shape[0] // lc
    vb = v.astype(BF16)
    pi = lax.shift_right_logical(lax.broadcasted_iota(jnp.int32, (lc, lc), 0), 6)
    pj = lax.shift_right_logical(lax.broadcasted_iota(jnp.int32, (lc, lc), 1), 6)
    mask = pj <= pi
    heads = []
    for h in range(A_HEADS):
        lanes = slice(h * A_HEAD_DIM, (h + 1) * A_HEAD_DIM)
        w = jnp.where(mask, ws_ref[h, 0:lc, 0:lc], 0.0).astype(BF16)
        vh = jnp.concatenate([vb[c * lc:(c + 1) * lc, lanes] for c in range(n_chunks)], axis=1)
        sh = jnp.dot(w, vh, preferred_element_type=F32)
        bias = bias_ref[h, 0:lc, :]
        heads.append(jnp.concatenate(
            [sh[:, c * A_HEAD_DIM:(c + 1) * A_HEAD_DIM] + bias for c in range(n_chunks)], axis=0))
    return jnp.concatenate(heads, axis=1)


def _mixer_in(x, gpre_ref, win_ref):
    hn = _rmsnorm(x, gpre_ref[...]).astype(BF16)
    return jnp.dot(hn, win_ref[...], preferred_element_type=F32)


def _mlp_up(x, gpre_ref, up_ref):
    hn = _rmsnorm(x, gpre_ref[...]).astype(BF16)
    return jnp.dot(hn, up_ref[...], preferred_element_type=F32)


def _mlp_down(x, hid, down_ref, gpost_ref):
    act = jnp.square(jnp.maximum(hid, 0.0)).astype(BF16)
    f = jnp.dot(act, down_ref[...], preferred_element_type=F32)
    return x + _rmsnorm(f, gpost_ref[...])


def _keep_state(cfg, new, old):
    return jnp.where(pl.program_id(0) < cfg.n_steps, new, old) if cfg.skewed else new


def _even_mixer(cfg, x, z, refs, states, tile, between):
    (vgain_ref, ws_ref, bias_ref, convw_ref, convb_ref, wgate_ref, ba_ref, bx_ref, lam_ref, wout_ref,
     gpost_ref) = refs
    hist_scr, h_scr, tailx_ref, tailh_ref, v_ref = states
    srows = cfg.tile_state(tile)
    u = jax.nn.gelu(z[:, 0:D_A])
    v = _rmsnorm(jax.nn.gelu(z[:, D_A:2 * D_A]), vgain_ref[...])
    xb = z[:, 2 * D_A:2 * D_A + D_B]
    gb = z[:, 2 * D_A + D_B:]
    if v_ref is not None:
        v_ref[tile * ROW_TILE:(tile + 1) * ROW_TILE, :] = v
    hist = hist_scr[srows, :]
    xc = _causal_conv(xb, hist, convw_ref[...], cfg) + convb_ref[...]
    new_hist = _keep_state(cfg, _seq_tails(xb, cfg), hist)
    hist_scr[srows, :] = new_hist
    tailx_ref[srows, :] = new_hist

    gate = _spatial_gating(v, ws_ref, bias_ref, cfg)
    half = D_B // 2
    xcb = xc.astype(BF16)
    g0 = jnp.dot(xcb[:, :half], wgate_ref[0], preferred_element_type=F32)
    g1 = jnp.dot(xcb[:, half:], wgate_ref[1], preferred_element_type=F32)
    between()

    a_out = u * gate
    r = jax.nn.sigmoid(jnp.concatenate([g0[:, :half], g1[:, :half]], axis=1) + ba_ref[...])
    ig = jax.nn.sigmoid(jnp.concatenate([g0[:, half:], g1[:, half:]], axis=1) + bx_ref[...])
    nl = -lam_ref[...]
    softplus = jnp.maximum(nl, 0.0) + jnp.log1p(jnp.exp(-jnp.abs(nl)))
    log_a = -LRU_C * r * softplus
    a = jnp.exp(log_a)
    bterm = jnp.sqrt(1.0 - a * a) * ig * xc
    h_prev = h_scr[srows, :]
    hseq = _linear_scan(a, bterm, h_prev, cfg)
    new_h = _keep_state(cfg, _seq_tails(hseq, cfg), h_prev)
    h_scr[srows, :] = new_h
    tailh_ref[srows, :] = new_h

    b_out = hseq * jax.nn.gelu(gb)
    mix = jnp.concatenate([a_out, b_out], axis=1).astype(BF16)
    m = jnp.dot(mix, wout_ref[...], preferred_element_type=F32)
    return x + _rmsnorm(m, gpost_ref[...])


def _odd_mixer(cfg, x, z, refs, states, tile, between):
    convw_ref, wout_ref, gpost_ref = refs
    hist_scr, tail_ref = states
    srows = cfg.tile_state(tile)
    between()
    bg = z[:, 0:D_C]
    p = z[:, D_C:2 * D_C] * z[:, 2 * D_C:]
    hist = hist_scr[srows, :]
    y = _causal_conv(p, hist, convw_ref[...], cfg)
    new_hist = _keep_state(cfg, _seq_tails(p, cfg), hist)
    hist_scr[srows, :] = new_hist
    tail_ref[srows, :] = new_hist
    m = jnp.dot((bg * y).astype(BF16), wout_ref[...], preferred_element_type=F32)
    return x + _rmsnorm(m, gpost_ref[...])


def _paired(mixer, x, z_of, mlp_refs, mlp_x, write_mlp):
    gfpre_ref, up_ref, down_ref, gfpost_ref = mlp_refs
    z = z_of(x) if x is not None else None
    hid = _mlp_up(mlp_x, gfpre_ref, up_ref) if mlp_x is not None else None

    def mlp_down():
        if mlp_x is not None:
            write_mlp(_mlp_down(mlp_x, hid, down_ref, gfpost_ref))

    if x is None:
        mlp_down()
        return None
    return mixer(x, z, mlp_down)


def _layer_body(cfg, mixer, z_of, mlp_refs, x_ref, xo_ref, x1_scr):
    rows = lambda t: slice(t * ROW_TILE, (t + 1) * ROW_TILE)

    def write_out(t):
        def write(val):
            xo_ref[rows(t), :] = val
        return write

    if not cfg.skewed:
        x1_prev = None
        for t in range(TILES_PER_STEP + 1):
            x = x_ref[rows(t), :] if t < TILES_PER_STEP else None
            x1 = _paired(functools.partial(mixer, t), x, z_of, mlp_refs, x1_prev,
                         write_out(t - 1) if t > 0 else None)
            x1_prev = x1
        return

    @pl.when(pl.program_id(0) == 0)
    def _():
        x1_scr[...] = jnp.zeros_like(x1_scr)

    for t in range(TILES_PER_STEP):
        x1_scr[rows(t), :] = _paired(functools.partial(mixer, t), x_ref[rows(t), :], z_of, mlp_refs,
                                     x1_scr[rows(t), :], write_out(t))


def _load_states_at_sequence_start(cfg, pairs):
    i = pl.program_id(0)

    @pl.when(jnp.logical_and(lax.rem(i, cfg.steps_per_seq) == 0, i < cfg.n_steps))
    def _():
        for scr, src in pairs:
            scr[...] = src[...]


def _cast_next_layer_weights(cast_in_refs, cast_out_refs):
    for src, dst in zip(cast_in_refs, cast_out_refs):
        dst[...] = src[...].astype(BF16)


N_EVEN_CONSTS = 17
N_ODD_CONSTS = 9


def _even_layer_kernel(cfg, n_cast, *refs):
    refs = list(refs)
    x_ref = refs.pop(0)
    (gpre_ref, win_ref, vgain_ref, ws_ref, bias_ref, convw_ref, convb_ref, wgate_ref, ba_ref, bx_ref, lam_ref,
     wout_ref, gpost_ref, gfpre_ref, up_ref, down_ref, gfpost_ref) = refs[:N_EVEN_CONSTS]
    del refs[:N_EVEN_CONSTS]
    hist_in_ref, h_in_ref = refs[:2]
    cast_in_refs = refs[2:2 + n_cast]
    del refs[:2 + n_cast]
    xo_ref, tailx_ref, tailh_ref = refs[:3]
    del refs[:3]
    v_ref = refs.pop(0) if cfg.emit_v else None
    cast_out_refs = refs[:n_cast]
    del refs[:n_cast]
    hist_scr, h_scr = refs[:2]
    x1_scr = refs[2] if cfg.skewed else None

    mixer_refs = (vgain_ref, ws_ref, bias_ref, convw_ref, convb_ref, wgate_ref, ba_ref, bx_ref, lam_ref,
                  wout_ref, gpost_ref)
    states = (hist_scr, h_scr, tailx_ref, tailh_ref, v_ref)
    _load_states_at_sequence_start(cfg, [(hist_scr, hist_in_ref), (h_scr, h_in_ref)])

    def mixer(tile, x, z, between):
        return _even_mixer(cfg, x, z, mixer_refs, states, tile, between)

    _layer_body(cfg, mixer, lambda x: _mixer_in(x, gpre_ref, win_ref), (gfpre_ref, up_ref, down_ref, gfpost_ref),
                x_ref, xo_ref, x1_scr)
    _cast_next_layer_weights(cast_in_refs, cast_out_refs)


def _odd_layer_kernel(cfg, n_cast, *refs):
    refs = list(refs)
    x_ref = refs.pop(0)
    (gpre_ref, win_ref, convw_ref, wout_ref, gpost_ref, gfpre_ref, up_ref, down_ref,
     gfpost_ref) = refs[:N_ODD_CONSTS]
    del refs[:N_ODD_CONSTS]
    hist_in_ref = refs.pop(0)
    cast_in_refs = refs[:n_cast]
    del refs[:n_cast]
    xo_ref, tail_ref = refs[:2]
    del refs[:2]
    cast_out_refs = refs[:n_cast]
    del refs[:n_cast]
    hist_scr = refs[0]
    x1_scr = refs[1] if cfg.skewed else None

    _load_states_at_sequence_start(cfg, [(hist_scr, hist_in_ref)])

    def mixer(tile, x, z, between):
        return _odd_mixer(cfg, x, z, (convw_ref, wout_ref, gpost_ref), (hist_scr, tail_ref), tile, between)

    _layer_body(cfg, mixer, lambda x: _mixer_in(x, gpre_ref, win_ref), (gfpre_ref, up_ref, down_ref, gfpost_ref),
                x_ref, xo_ref, x1_scr)
    _cast_next_layer_weights(cast_in_refs, cast_out_refs)


def _resident(arr, layer):
    if layer is None:
        zeros = (0,) * arr.ndim
        return pl.BlockSpec(arr.shape, lambda i: zeros, pipeline_mode=pl.Buffered(1))
    zeros = (0,) * (arr.ndim - 1)
    return pl.BlockSpec((None,) + arr.shape[1:], lambda i: (layer,) + zeros, pipeline_mode=pl.Buffered(1))


def _cast_specs(cfg, cast_src):
    in_specs, out_specs, out_shape = [], [], []
    for arr, k in cast_src:
        _, n_rows, n_cols = arr.shape
        chunk = n_rows // cfg.n_steps
        in_specs.append(pl.BlockSpec((None, chunk, n_cols), lambda i, _k=k: (_k, _mixer_step(cfg, i), 0)))
        out_specs.append(pl.BlockSpec((chunk, n_cols), lambda i: (_mixer_step(cfg, i), 0)))
        out_shape.append(jax.ShapeDtypeStruct((n_rows, n_cols), BF16))
    return in_specs, out_specs, out_shape


def _mixer_step(cfg, i):
    return jnp.minimum(i, cfg.n_steps - 1)


def _state_spec(cfg, width):
    return pl.BlockSpec((cfg.state_rows, width), lambda i: (_mixer_step(cfg, i) // cfg.steps_per_seq, 0))


def _row_in_spec(cfg, width):
    return pl.BlockSpec((STEP_ROWS, width), lambda i: (_mixer_step(cfg, i), 0))


def _row_out_spec(cfg, width):
    if cfg.skewed:
        return pl.BlockSpec((STEP_ROWS, width), lambda i: (jnp.maximum(i - 1, 0), 0))
    return pl.BlockSpec((STEP_ROWS, width), lambda i: (i, 0))


def _grid(cfg):
    return (cfg.n_steps + 1,) if cfg.skewed else (cfg.n_steps,)


def _skew_scratch(cfg):
    return [pltpu.VMEM((STEP_ROWS, D_MODEL), F32)] if cfg.skewed else []


def _compiler_params():
    return pltpu.CompilerParams(dimension_semantics=("arbitrary",), vmem_limit_bytes=VMEM_LIMIT_BYTES)


def _even_layer(cfg, x, hist, h0, w, big, layer, cast_src=()):
    rows = x.shape[0]
    n_state_rows = hist.shape[0]
    e = layer // 2
    consts = [(w["gpre"], layer), (big["win"], None), (w["vgain"], e), (w["ws"], e), (w["bias"], e),
              (w["convw"], e), (w["convb"], e), (w["wgate"], e), (w["ba"], e), (w["bx"], e), (w["lam"], e),
              (big["wout"], None), (w["gpost"], layer), (w["gfpre"], layer), (big["up"], None),
              (big["down"], None), (w["gfpost"], layer)]
    assert len(consts) == N_EVEN_CONSTS
    cast_in, cast_out, cast_shape = _cast_specs(cfg, cast_src)
    in_specs = ([_row_in_spec(cfg, D_MODEL)] + [_resident(c, k) for c, k in consts]
                + [_state_spec(cfg, D_B), _state_spec(cfg, D_B)] + cast_in)
    out_shape = [jax.ShapeDtypeStruct((rows, D_MODEL), F32),
                 jax.ShapeDtypeStruct((n_state_rows, D_B), F32),
                 jax.ShapeDtypeStruct((n_state_rows, D_B), F32)]
    out_specs = [_row_out_spec(cfg, D_MODEL), _state_spec(cfg, D_B), _state_spec(cfg, D_B)]
    if cfg.emit_v:
        out_shape.append(jax.ShapeDtypeStruct((rows, D_A), F32))
        out_specs.append(_row_in_spec(cfg, D_A))
    scratch = ([pltpu.VMEM((cfg.state_rows, D_B), F32), pltpu.VMEM((cfg.state_rows, D_B), F32)]
               + _skew_scratch(cfg))
    return pl.pallas_call(
        functools.partial(_even_layer_kernel, cfg, len(cast_src)),
        grid=_grid(cfg),
        in_specs=in_specs,
        out_specs=out_specs + cast_out,
        out_shape=out_shape + cast_shape,
        scratch_shapes=scratch,
        compiler_params=_compiler_params(),
        name="even_layer",
    )(x, *[c for c, _ in consts], hist, h0, *[arr for arr, _ in cast_src])


def _odd_layer(cfg, x, hist, w, big, layer, cast_src=()):
    rows = x.shape[0]
    o = layer // 2
    consts = [(w["gpre"], layer), (big["win"], None), (w["cconvw"], o), (big["wout"], None), (w["gpost"], layer),
              (w["gfpre"], layer), (big["up"], None), (big["down"], None), (w["gfpost"], layer)]
    assert len(consts) == N_ODD_CONSTS
    cast_in, cast_out, cast_shape = _cast_specs(cfg, cast_src)
    in_specs = ([_row_in_spec(cfg, D_MODEL)] + [_resident(c, k) for c, k in consts]
                + [_state_spec(cfg, D_C)] + cast_in)
    out_shape = [jax.ShapeDtypeStruct((rows, D_MODEL), F32),
                 jax.ShapeDtypeStruct((hist.shape[0], D_C), F32)]
    out_specs = [_row_out_spec(cfg, D_MODEL), _state_spec(cfg, D_C)]
    scratch = [pltpu.VMEM((cfg.state_rows, D_C), F32)] + _skew_scratch(cfg)
    return pl.pallas_call(
        functools.partial(_odd_layer_kernel, cfg, len(cast_src)),
        grid=_grid(cfg),
        in_specs=in_specs,
        out_specs=out_specs + cast_out,
        out_shape=out_shape + cast_shape,
        scratch_shapes=scratch,
        compiler_params=_compiler_params(),
        name="odd_layer",
    )(x, *[c for c, _ in consts], hist, *[arr for arr, _ in cast_src])


def _block_diag_gates(wa, wx):
    heads_per_half = B_HEADS // 2
    eye = jnp.eye(heads_per_half, dtype=wa.dtype)

    def bd(w):
        return jnp.einsum("eqhij,hg->eqhigj", w, eye).reshape(w.shape[0], 2, D_B // 2, D_B // 2)

    split = lambda w: w.reshape(w.shape[0], 2, heads_per_half, B_HEAD_DIM, B_HEAD_DIM)
    return jnp.concatenate([bd(split(wa)), bd(split(wx))], axis=-1).astype(BF16)


def _pad_state(rows):
    n, r, c = rows.shape
    return jnp.pad(rows, ((0, 0), (SUBLANES - r, 0), (0, 0))).reshape(n * SUBLANES, c)


def _small_weights(p):
    row = lambda v: v[:, None, :]
    a_b_s = p["a_b_s"]
    return dict(
        gpre=row(p["norm_mix_pre"]), gpost=row(p["norm_mix_post"]), gfpre=row(p["norm_ffn_pre"]),
        gfpost=row(p["norm_ffn_post"]), vgain=row(p["a_v_gain"]), ws=p["a_w_s"],
        bias=jnp.broadcast_to(a_b_s[..., None], a_b_s.shape + (A_HEAD_DIM,)), convw=p["b_conv_w"],
        convb=row(p["b_conv_b"]), wgate=_block_diag_gates(p["b_wa"], p["b_wx"]), ba=row(p["b_ba"]),
        bx=row(p["b_bx"]), lam=row(p["b_lambda"]), cconvw=p["c_conv_w"])


BIG_NAMES = ("win", "wout", "up", "down")


def _big_f32(p, layer):
    k = layer // 2
    if layer % 2 == 0:
        return [(p["w_in_even"], k), (p["w_out_even"], k), (p["mlp_up"], layer), (p["mlp_down"], layer)]
    return [(p["c_w_in"], k), (p["c_w_out"], k), (p["mlp_up"], layer), (p["mlp_down"], layer)]


def _trunk(x, b_conv, b_h, c_conv, small, big, cfg, p=None):
    tails_b, tails_h, tails_c, v_rows = [], [], [], []
    for l in range(DEPTH):
        cast_src = _big_f32(p, l + 1) if p is not None and l + 1 < DEPTH else ()
        if l % 2 == 0:
            outs = _even_layer(cfg, x, b_conv[l // 2], b_h[l // 2], small, big[l], l, cast_src)
            n_fixed = 4 if cfg.emit_v else 3
            tails_b.append(outs[1])
            tails_h.append(outs[2])
            if cfg.emit_v:
                v_rows.append(outs[3])
        else:
            outs = _odd_layer(cfg, x, c_conv[l // 2], small, big[l], l, cast_src)
            n_fixed = 2
            tails_c.append(outs[1])
        x = outs[0]
        if cast_src:
            big[l + 1] = dict(zip(BIG_NAMES, outs[n_fixed:]))
    return x, tails_b, tails_h, tails_c, v_rows


def kernel(x_prompt, x_sample, cache_b_conv, state_b_h, cache_c_conv, norm_mix_pre, norm_mix_post, norm_ffn_pre, norm_ffn_post, w_in_even, a_v_gain, a_w_s, a_b_s, b_conv_w, b_conv_b, b_wa, b_ba, b_wx, b_bx, b_lambda, w_out_even, c_w_in, c_conv_w, c_w_out, mlp_up, mlp_down):
    batch, seq, _ = x_prompt.shape
    dec_batch, dec_seq, _ = x_sample.shape
    n_even, n_odd = w_in_even.shape[0], c_w_in.shape[0]
    p = dict(norm_mix_pre=norm_mix_pre, norm_mix_post=norm_mix_post, norm_ffn_pre=norm_ffn_pre,
             norm_ffn_post=norm_ffn_post, w_in_even=w_in_even, a_v_gain=a_v_gain, a_w_s=a_w_s, a_b_s=a_b_s,
             b_conv_w=b_conv_w, b_conv_b=b_conv_b, b_wa=b_wa, b_wx=b_wx, b_ba=b_ba, b_bx=b_bx,
             b_lambda=b_lambda, w_out_even=w_out_even, c_w_in=c_w_in, c_conv_w=c_conv_w, c_w_out=c_w_out,
             mlp_up=mlp_up, mlp_down=mlp_down)
    small = _small_weights(p)
    big = {0: {name: arr[k].astype(BF16) for name, (arr, k) in zip(BIG_NAMES, _big_f32(p, 0))}}

    cfg_p = TileCfg(n_seq=1, seq_rows=ROW_TILE, steps_per_seq=seq // STEP_ROWS, gate_chunk=A_CHUNK,
                    emit_v=False, n_steps=batch * seq // STEP_ROWS, skewed=True)
    zb = jnp.zeros((n_even, batch * SUBLANES, D_B), F32)
    zc = jnp.zeros((n_odd, batch * SUBLANES, D_C), F32)
    y_p, tb_p, th_p, tc_p, _ = _trunk(x_prompt.reshape(batch * seq, D_MODEL), zb, zb, zc, small, big, cfg_p, p)

    cfg_s = TileCfg(n_seq=ROW_TILE // dec_seq, seq_rows=dec_seq, steps_per_seq=1, gate_chunk=dec_seq,
                    emit_v=True, n_steps=dec_batch * dec_seq // STEP_ROWS, skewed=False)
    sb = jnp.stack([_pad_state(cache_b_conv[e]) for e in range(n_even)])
    sh = jnp.stack([_pad_state(state_b_h[e][:, None, :]) for e in range(n_even)])
    sc = jnp.stack([_pad_state(cache_c_conv[o]) for o in range(n_odd)])
    y_s, tb_s, th_s, tc_s, v_s = _trunk(x_sample.reshape(dec_batch * dec_seq, D_MODEL), sb, sh, sc, small, big,
                                         cfg_s)

    def tails(ts, n, keep):
        t = jnp.stack(ts).reshape(len(ts), n, SUBLANES, -1)
        return t[:, :, SUBLANES - keep:, :]

    return (y_p.reshape(batch, seq, D_MODEL),
            y_s.reshape(dec_batch, dec_seq, D_MODEL),
            jnp.stack(v_s).reshape(n_even, dec_batch, dec_seq, D_A),
            tails(tb_p, batch, B_CONV - 1),
            tails(th_p, batch, 1)[:, :, 0, :],
            tails(tc_p, batch, C_CONV - 1),
            tails(tb_s, dec_batch, B_CONV - 1),
            tails(th_s, dec_batch, 1)[:, :, 0, :],
            tails(tc_s, dec_batch, C_CONV - 1))
```

```python
import functools
from typing import NamedTuple

import jax
import jax.numpy as jnp
from jax import lax
from jax.experimental import pallas as pl
from jax.experimental.pallas import tpu as pltpu

D_MODEL = 1024
DEPTH = 4
CHUNK = 64
A_CHUNK = 128
D_A = D_MODEL // 2
A_HEADS = 4
A_HEAD_DIM = D_A // A_HEADS
D_B = D_MODEL // 2
B_HEADS = 8
B_HEAD_DIM = D_B // B_HEADS
B_CONV = 4
LRU_C = 8.0
D_C = D_MODEL
C_CONV = 3
D_FF = 4 * D_MODEL
EPS = 1e-6

SUBLANES = 8
V7X_VMEM_BYTES = 64 * 1024 * 1024
VMEM_LIMIT_BYTES = V7X_VMEM_BYTES - 8 * 1024 * 1024
ROW_TILE = 512
TILES_PER_STEP = 1
STEP_ROWS = ROW_TILE * TILES_PER_STEP

F32 = jnp.float32
BF16 = jnp.bfloat16


class TileCfg(NamedTuple):
    n_seq: int
    seq_rows: int
    steps_per_seq: int
    gate_chunk: int
    emit_v: bool
    n_steps: int
    skewed: bool

    @property
    def state_rows(self):
        groups = TILES_PER_STEP if self.n_seq > 1 else 1
        return groups * self.n_seq * SUBLANES

    def tile_state(self, t):
        if self.n_seq == 1:
            return slice(0, SUBLANES)
        return slice(t * self.n_seq * SUBLANES, (t + 1) * self.n_seq * SUBLANES)


def _rmsnorm(x, g):
    ms = jnp.mean(x * x, axis=-1, keepdims=True)
    return x * lax.rsqrt(ms + EPS) * g


def _shift_rows(x, hist, k, cfg):
    if k == 0:
        return x
    rolled = pltpu.roll(x, k, 0)
    row = lax.broadcasted_iota(jnp.int32, (SUBLANES, x.shape[1]), 0)
    pieces = []
    for s in range(cfg.n_seq):
        lo = s * cfg.seq_rows
        h = pltpu.roll(hist[s * SUBLANES:(s + 1) * SUBLANES], k, 0)
        pieces.append(jnp.where(row < k, h, rolled[lo:lo + SUBLANES]))
        pieces.append(rolled[lo + SUBLANES:lo + cfg.seq_rows])
    return jnp.concatenate(pieces, axis=0)


def _causal_conv(x, hist, w, cfg):
    width = w.shape[0]
    y = w[0:1] * _shift_rows(x, hist, width - 1, cfg)
    for k in range(1, width):
        y = y + w[k:k + 1] * _shift_rows(x, hist, width - 1 - k, cfg)
    return y


def _seq_tails(x, cfg):
    pieces = [x[(s + 1) * cfg.seq_rows - SUBLANES:(s + 1) * cfg.seq_rows] for s in range(cfg.n_seq)]
    return pieces[0] if len(pieces) == 1 else jnp.concatenate(pieces, axis=0)


def _linear_scan(a, b, h_prev, cfg):
    row = lax.broadcasted_iota(jnp.int32, a.shape, 0) & (SUBLANES - 1)
    d = 1
    while d < SUBLANES:
        keep = row >= d
        a_sh = pltpu.roll(a, d, 0)
        b_sh = pltpu.roll(b, d, 0)
        b = jnp.where(keep, a * b_sh + b, b)
        a = jnp.where(keep, a * a_sh, a)
        d *= 2
    groups_per_seq = cfg.seq_rows // SUBLANES
    out = []
    for s in range(cfg.n_seq):
        carry = h_prev[s * SUBLANES + SUBLANES - 1:(s + 1) * SUBLANES]
        for g in range(groups_per_seq):
            lo = s * cfg.seq_rows + g * SUBLANES
            hg = b[lo:lo + SUBLANES] + a[lo:lo + SUBLANES] * carry
            out.append(hg)
            carry = hg[SUBLANES - 1:SUBLANES]
    return jnp.concatenate(out, axis=0)


def _spatial_gating(v, ws_ref, bias_ref, cfg):
    lc = cfg.gate_chunk
    n_chunks = v.shape[0] // lc
    vb = v.astype(BF16)
    pi = lax.shift_right_logical(lax.broadcasted_iota(jnp.int32, (lc, lc), 0), 6)
    pj = lax.shift_right_logical(lax.broadcasted_iota(jnp.int32, (lc, lc), 1), 6)
    mask = pj <= pi
    heads = []
    for h in range(A_HEADS):
        lanes = slice(h * A_HEAD_DIM, (h + 1) * A_HEAD_DIM)
        w = jnp.where(mask, ws_ref[h, 0:lc, 0:lc], 0.0).astype(BF16)
        vh = jnp.concatenate([vb[c * lc:(c + 1) * lc, lanes] for c in range(n_chunks)], axis=1)
        sh = jnp.dot(w, vh, preferred_element_type=F32)
        bias = bias_ref[h, 0:lc, :]
        heads.append(jnp.concatenate(
            [sh[:, c * A_HEAD_DIM:(c + 1) * A_HEAD_DIM] + bias for c in range(n_chunks)], axis=0))
    return jnp.concatenate(heads, axis=1)


def _mixer_in(x, gpre_ref, win_ref):
    hn = _rmsnorm(x, gpre_ref[...]).astype(BF16)
    return jnp.dot(hn, win_ref[...], preferred_element_type=F32)


def _mixer_out(x, mix, wout_ref, gpost_ref):
    m = jnp.dot(mix, wout_ref[...], preferred_element_type=F32)
    return x + _rmsnorm(m, gpost_ref[...])


def _mlp_up(x, gpre_ref, up_ref):
    hn = _rmsnorm(x, gpre_ref[...]).astype(BF16)
    return jnp.dot(hn, up_ref[...], preferred_element_type=F32)


def _mlp_down(x, hid, down_ref, gpost_ref):
    act = jnp.square(jnp.maximum(hid, 0.0)).astype(BF16)
    f = jnp.dot(act, down_ref[...], preferred_element_type=F32)
    return x + _rmsnorm(f, gpost_ref[...])


def _keep_state(cfg, new, old):
    return jnp.where(pl.program_id(0) < cfg.n_steps, new, old) if cfg.skewed else new


def _even_mixer(cfg, x, z, refs, states, tile, between):
    (vgain_ref, ws_ref, bias_ref, convw_ref, convb_ref, wgate_ref, ba_ref, bx_ref, lam_ref, wout_ref,
     gpost_ref) = refs
    hist_scr, h_scr, tailx_ref, tailh_ref, v_ref = states
    srows = cfg.tile_state(tile)
    u = jax.nn.gelu(z[:, 0:D_A])
    v = _rmsnorm(jax.nn.gelu(z[:, D_A:2 * D_A]), vgain_ref[...])
    xb = z[:, 2 * D_A:2 * D_A + D_B]
    gb = z[:, 2 * D_A + D_B:]
    if v_ref is not None:
        v_ref[tile * ROW_TILE:(tile + 1) * ROW_TILE, :] = v
    hist = hist_scr[srows, :]
    xc = _causal_conv(xb, hist, convw_ref[...], cfg) + convb_ref[...]
    new_hist = _keep_state(cfg, _seq_tails(xb, cfg), hist)
    hist_scr[srows, :] = new_hist
    tailx_ref[srows, :] = new_hist

    gate = _spatial_gating(v, ws_ref, bias_ref, cfg)
    half = D_B // 2
    xcb = xc.astype(BF16)
    g0 = jnp.dot(xcb[:, :half], wgate_ref[0], preferred_element_type=F32)
    g1 = jnp.dot(xcb[:, half:], wgate_ref[1], preferred_element_type=F32)
    between()

    a_out = u * gate
    r = jax.nn.sigmoid(jnp.concatenate([g0[:, :half], g1[:, :half]], axis=1) + ba_ref[...])
    ig = jax.nn.sigmoid(jnp.concatenate([g0[:, half:], g1[:, half:]], axis=1) + bx_ref[...])
    nl = -lam_ref[...]
    softplus = jnp.maximum(nl, 0.0) + jnp.log1p(jnp.exp(-jnp.abs(nl)))
    log_a = -LRU_C * r * softplus
    a = jnp.exp(log_a)
    bterm = jnp.sqrt(-jnp.tanh(log_a) * (a * a + 1.0)) * ig * xc
    h_prev = h_scr[srows, :]
    hseq = _linear_scan(a, bterm, h_prev, cfg)
    new_h = _keep_state(cfg, _seq_tails(hseq, cfg), h_prev)
    h_scr[srows, :] = new_h
    tailh_ref[srows, :] = new_h

    b_out = hseq * jax.nn.gelu(gb)
    mix = jnp.concatenate([a_out, b_out], axis=1).astype(BF16)
    return _mixer_out(x, mix, wout_ref, gpost_ref)


def _odd_mixer(cfg, x, z, refs, states, tile, between):
    convw_ref, wout_ref, gpost_ref = refs
    hist_scr, tail_ref = states
    srows = cfg.tile_state(tile)
    between()
    bg = z[:, 0:D_C]
    p = z[:, D_C:2 * D_C] * z[:, 2 * D_C:]
    hist = hist_scr[srows, :]
    y = _causal_conv(p, hist, convw_ref[...], cfg)
    new_hist = _keep_state(cfg, _seq_tails(p, cfg), hist)
    hist_scr[srows, :] = new_hist
    tail_ref[srows, :] = new_hist
    return _mixer_out(x, (bg * y).astype(BF16), wout_ref, gpost_ref)


def _paired(mixer, x, z_of, mlp_refs, mlp_x, write_mlp):
    gfpre_ref, up_ref, down_ref, gfpost_ref = mlp_refs
    z = z_of(x) if x is not None else None
    hid = _mlp_up(mlp_x, gfpre_ref, up_ref) if mlp_x is not None else None

    def mlp_down():
        if mlp_x is not None:
            write_mlp(_mlp_down(mlp_x, hid, down_ref, gfpost_ref))

    if x is None:
        mlp_down()
        return None
    return mixer(x, z, mlp_down)


def _layer_body(cfg, mixer, z_of, mlp_refs, x_ref, xo_ref, x1_scr):
    rows = lambda t: slice(t * ROW_TILE, (t + 1) * ROW_TILE)

    def write_out(t):
        def write(val):
            xo_ref[rows(t), :] = val
        return write

    if not cfg.skewed:
        x1_prev = None
        for t in range(TILES_PER_STEP + 1):
            x = x_ref[rows(t), :] if t < TILES_PER_STEP else None
            x1 = _paired(functools.partial(mixer, t), x, z_of, mlp_refs, x1_prev,
                         write_out(t - 1) if t > 0 else None)
            x1_prev = x1
        return

    @pl.when(pl.program_id(0) == 0)
    def _():
        x1_scr[...] = jnp.zeros_like(x1_scr)

    for t in range(TILES_PER_STEP):
        x1_scr[rows(t), :] = _paired(functools.partial(mixer, t), x_ref[rows(t), :], z_of, mlp_refs,
                                     x1_scr[rows(t), :], write_out(t))


def _load_states_at_sequence_start(cfg, pairs):
    i = pl.program_id(0)

    @pl.when(jnp.logical_and(lax.rem(i, cfg.steps_per_seq) == 0, i < cfg.n_steps))
    def _():
        for scr, src in pairs:
            scr[...] = src[...]


def _cast_next_layer_weights(cast_in_refs, cast_out_refs):
    for src, dst in zip(cast_in_refs, cast_out_refs):
        dst[...] = src[...].astype(BF16)


N_EVEN_CONSTS = 17
N_ODD_CONSTS = 9
BIG_NAMES = ("win", "wout", "up", "down")


def _even_layer_kernel(cfg, n_cast, *refs):
    refs = list(refs)
    x_ref = refs.pop(0)
    (gpre_ref, win_ref, vgain_ref, ws_ref, bias_ref, convw_ref, convb_ref, wgate_ref, ba_ref, bx_ref, lam_ref,
     wout_ref, gpost_ref, gfpre_ref, up_ref, down_ref, gfpost_ref) = refs[:N_EVEN_CONSTS]
    del refs[:N_EVEN_CONSTS]
    hist_in_ref, h_in_ref = refs[:2]
    cast_in_refs = refs[2:2 + n_cast]
    del refs[:2 + n_cast]
    xo_ref, tailx_ref, tailh_ref = refs[:3]
    del refs[:3]
    v_ref = refs.pop(0) if cfg.emit_v else None
    cast_out_refs = refs[:n_cast]
    del refs[:n_cast]
    hist_scr, h_scr = refs[:2]
    x1_scr = refs[2] if cfg.skewed else None

    mixer_refs = (vgain_ref, ws_ref, bias_ref, convw_ref, convb_ref, wgate_ref, ba_ref, bx_ref, lam_ref,
                  wout_ref, gpost_ref)
    states = (hist_scr, h_scr, tailx_ref, tailh_ref, v_ref)
    _load_states_at_sequence_start(cfg, [(hist_scr, hist_in_ref), (h_scr, h_in_ref)])

    def mixer(tile, x, z, between):
        return _even_mixer(cfg, x, z, mixer_refs, states, tile, between)

    _layer_body(cfg, mixer, lambda x: _mixer_in(x, gpre_ref, win_ref), (gfpre_ref, up_ref, down_ref, gfpost_ref),
                x_ref, xo_ref, x1_scr)
    _cast_next_layer_weights(cast_in_refs, cast_out_refs)


def _odd_layer_kernel(cfg, n_cast, *refs):
    refs = list(refs)
    x_ref = refs.pop(0)
    (gpre_ref, win_ref, convw_ref, wout_ref, gpost_ref, gfpre_ref, up_ref, down_ref,
     gfpost_ref) = refs[:N_ODD_CONSTS]
    del refs[:N_ODD_CONSTS]
    hist_in_ref = refs.pop(0)
    cast_in_refs = refs[:n_cast]
    del refs[:n_cast]
    xo_ref, tail_ref = refs[:2]
    del refs[:2]
    cast_out_refs = refs[:n_cast]
    del refs[:n_cast]
    hist_scr = refs[0]
    x1_scr = refs[1] if cfg.skewed else None

    _load_states_at_sequence_start(cfg, [(hist_scr, hist_in_ref)])

    def mixer(tile, x, z, between):
        return _odd_mixer(cfg, x, z, (convw_ref, wout_ref, gpost_ref), (hist_scr, tail_ref), tile, between)

    _layer_body(cfg, mixer, lambda x: _mixer_in(x, gpre_ref, win_ref), (gfpre_ref, up_ref, down_ref, gfpost_ref),
                x_ref, xo_ref, x1_scr)
    _cast_next_layer_weights(cast_in_refs, cast_out_refs)


def _resident(arr, layer):
    if layer is None:
        zeros = (0,) * arr.ndim
        return pl.BlockSpec(arr.shape, lambda i: zeros, pipeline_mode=pl.Buffered(1))
    zeros = (0,) * (arr.ndim - 1)
    return pl.BlockSpec((None,) + arr.shape[1:], lambda i: (layer,) + zeros, pipeline_mode=pl.Buffered(1))


def _cast_specs(cfg, cast_src):
    in_specs, out_specs, out_shape = [], [], []
    for arr, k in cast_src:
        _, n_rows, n_cols = arr.shape
        chunk = n_rows // cfg.n_steps
        in_specs.append(pl.BlockSpec((None, chunk, n_cols), lambda i, _k=k: (_k, _mixer_step(cfg, i), 0)))
        out_specs.append(pl.BlockSpec((chunk, n_cols), lambda i: (_mixer_step(cfg, i), 0)))
        out_shape.append(jax.ShapeDtypeStruct((n_rows, n_cols), BF16))
    return in_specs, out_specs, out_shape


def _mixer_step(cfg, i):
    return jnp.minimum(i, cfg.n_steps - 1)


def _state_in_spec(cfg, width, k):
    return pl.BlockSpec((None, cfg.state_rows, width),
                        lambda i: (k, _mixer_step(cfg, i) // cfg.steps_per_seq, 0))


def _state_spec(cfg, width):
    return pl.BlockSpec((cfg.state_rows, width), lambda i: (_mixer_step(cfg, i) // cfg.steps_per_seq, 0))


def _row_in_spec(cfg, width):
    return pl.BlockSpec((STEP_ROWS, width), lambda i: (_mixer_step(cfg, i), 0))


def _row_out_spec(cfg, width):
    if cfg.skewed:
        return pl.BlockSpec((STEP_ROWS, width), lambda i: (jnp.maximum(i - 1, 0), 0))
    return pl.BlockSpec((STEP_ROWS, width), lambda i: (i, 0))


def _grid(cfg):
    return (cfg.n_steps + 1,) if cfg.skewed else (cfg.n_steps,)


def _skew_scratch(cfg):
    return [pltpu.VMEM((STEP_ROWS, D_MODEL), F32)] if cfg.skewed else []


def _compiler_params():
    return pltpu.CompilerParams(dimension_semantics=("arbitrary",), vmem_limit_bytes=VMEM_LIMIT_BYTES)


def _even_layer(cfg, x, hist, h0, w, big, layer, cast_src=()):
    rows = x.shape[0]
    n_state_rows = hist.shape[1]
    e = layer // 2
    consts = [(w["gpre"], layer), (big["win"], None), (w["vgain"], e), (w["ws"], e), (w["bias"], e),
              (w["convw"], e), (w["convb"], e), (w["wgate"], e), (w["ba"], e), (w["bx"], e), (w["lam"], e),
              (big["wout"], None), (w["gpost"], layer), (w["gfpre"], layer), (big["up"], None),
              (big["down"], None), (w["gfpost"], layer)]
    assert len(consts) == N_EVEN_CONSTS
    cast_in, cast_out, cast_shape = _cast_specs(cfg, cast_src)
    in_specs = ([_row_in_spec(cfg, D_MODEL)] + [_resident(c, k) for c, k in consts]
                + [_state_in_spec(cfg, D_B, e), _state_in_spec(cfg, D_B, e)] + cast_in)
    out_shape = [jax.ShapeDtypeStruct((rows, D_MODEL), F32),
                 jax.ShapeDtypeStruct((n_state_rows, D_B), F32),
                 jax.ShapeDtypeStruct((n_state_rows, D_B), F32)]
    out_specs = [_row_out_spec(cfg, D_MODEL), _state_spec(cfg, D_B), _state_spec(cfg, D_B)]
    if cfg.emit_v:
        out_shape.append(jax.ShapeDtypeStruct((rows, D_A), F32))
        out_specs.append(_row_in_spec(cfg, D_A))
    scratch = ([pltpu.VMEM((cfg.state_rows, D_B), F32), pltpu.VMEM((cfg.state_rows, D_B), F32)]
               + _skew_scratch(cfg))
    return pl.pallas_call(
        functools.partial(_even_layer_kernel, cfg, len(cast_src)),
        grid=_grid(cfg),
        in_specs=in_specs,
        out_specs=out_specs + cast_out,
        out_shape=out_shape + cast_shape,
        scratch_shapes=scratch,
        compiler_params=_compiler_params(),
        name="even_layer",
    )(x, *[c for c, _ in consts], hist, h0, *[arr for arr, _ in cast_src])


def _odd_layer(cfg, x, hist, w, big, layer, cast_src=()):
    rows = x.shape[0]
    o = layer // 2
    consts = [(w["gpre"], layer), (big["win"], None), (w["cconvw"], o), (big["wout"], None), (w["gpost"], layer),
              (w["gfpre"], layer), (big["up"], None), (big["down"], None), (w["gfpost"], layer)]
    assert len(consts) == N_ODD_CONSTS
    cast_in, cast_out, cast_shape = _cast_specs(cfg, cast_src)
    in_specs = ([_row_in_spec(cfg, D_MODEL)] + [_resident(c, k) for c, k in consts]
                + [_state_in_spec(cfg, D_C, o)] + cast_in)
    out_shape = [jax.ShapeDtypeStruct((rows, D_MODEL), F32),
                 jax.ShapeDtypeStruct((hist.shape[1], D_C), F32)]
    out_specs = [_row_out_spec(cfg, D_MODEL), _state_spec(cfg, D_C)]
    scratch = [pltpu.VMEM((cfg.state_rows, D_C), F32)] + _skew_scratch(cfg)
    return pl.pallas_call(
        functools.partial(_odd_layer_kernel, cfg, len(cast_src)),
        grid=_grid(cfg),
        in_specs=in_specs,
        out_specs=out_specs + cast_out,
        out_shape=out_shape + cast_shape,
        scratch_shapes=scratch,
        compiler_params=_compiler_params(),
        name="odd_layer",
    )(x, *[c for c, _ in consts], hist, *[arr for arr, _ in cast_src])


def _block_diag_gates(wa, wx):
    heads_per_half = B_HEADS // 2
    eye = jnp.eye(heads_per_half, dtype=wa.dtype)

    def bd(w):
        return jnp.einsum("eqhij,hg->eqhigj", w, eye).reshape(w.shape[0], 2, D_B // 2, D_B // 2)

    split = lambda w: w.reshape(w.shape[0], 2, heads_per_half, B_HEAD_DIM, B_HEAD_DIM)
    return jnp.concatenate([bd(split(wa)), bd(split(wx))], axis=-1).astype(BF16)


def _pad_state(rows):
    layers, n, r, c = rows.shape
    return jnp.pad(rows, ((0, 0), (0, 0), (SUBLANES - r, 0), (0, 0))).reshape(layers, n * SUBLANES, c)


def _small_weights(p):
    row = lambda v: v[:, None, :]
    a_b_s = p["a_b_s"]
    return dict(
        gpre=row(p["norm_mix_pre"]), gpost=row(p["norm_mix_post"]), gfpre=row(p["norm_ffn_pre"]),
        gfpost=row(p["norm_ffn_post"]), vgain=row(p["a_v_gain"]), ws=p["a_w_s"],
        bias=jnp.broadcast_to(a_b_s[..., None], a_b_s.shape + (A_HEAD_DIM,)), convw=p["b_conv_w"],
        convb=row(p["b_conv_b"]), wgate=_block_diag_gates(p["b_wa"], p["b_wx"]), ba=row(p["b_ba"]),
        bx=row(p["b_bx"]), lam=row(p["b_lambda"]), cconvw=p["c_conv_w"])


def _big_f32(p, layer):
    k = layer // 2
    if layer % 2 == 0:
        return [(p["w_in_even"], k), (p["w_out_even"], k), (p["mlp_up"], layer), (p["mlp_down"], layer)]
    return [(p["c_w_in"], k), (p["c_w_out"], k), (p["mlp_up"], layer), (p["mlp_down"], layer)]


def _trunk(x, b_conv, b_h, c_conv, small, big, cfg, p=None):
    tails_b, tails_h, tails_c, v_rows = [], [], [], []
    for l in range(DEPTH):
        cast_src = _big_f32(p, l + 1) if p is not None and l + 1 < DEPTH else ()
        if l % 2 == 0:
            outs = _even_layer(cfg, x, b_conv, b_h, small, big[l], l, cast_src)
            n_fixed = 4 if cfg.emit_v else 3
            tails_b.append(outs[1])
            tails_h.append(outs[2])
            if cfg.emit_v:
                v_rows.append(outs[3])
        else:
            outs = _odd_layer(cfg, x, c_conv, small, big[l], l, cast_src)
            n_fixed = 2
            tails_c.append(outs[1])
        x = outs[0]
        if cast_src:
            big[l + 1] = dict(zip(BIG_NAMES, outs[n_fixed:]))
    return x, tails_b, tails_h, tails_c, v_rows


def kernel(x_prompt, x_sample, cache_b_conv, state_b_h, cache_c_conv, norm_mix_pre, norm_mix_post, norm_ffn_pre, norm_ffn_post, w_in_even, a_v_gain, a_w_s, a_b_s, b_conv_w, b_conv_b, b_wa, b_ba, b_wx, b_bx, b_lambda, w_out_even, c_w_in, c_conv_w, c_w_out, mlp_up, mlp_down):
    batch, seq, _ = x_prompt.shape
    dec_batch, dec_seq, _ = x_sample.shape
    n_even, n_odd = w_in_even.shape[0], c_w_in.shape[0]
    p = dict(norm_mix_pre=norm_mix_pre, norm_mix_post=norm_mix_post, norm_ffn_pre=norm_ffn_pre,
             norm_ffn_post=norm_ffn_post, w_in_even=w_in_even, a_v_gain=a_v_gain, a_w_s=a_w_s, a_b_s=a_b_s,
             b_conv_w=b_conv_w, b_conv_b=b_conv_b, b_wa=b_wa, b_wx=b_wx, b_ba=b_ba, b_bx=b_bx,
             b_lambda=b_lambda, w_out_even=w_out_even, c_w_in=c_w_in, c_conv_w=c_conv_w, c_w_out=c_w_out,
             mlp_up=mlp_up, mlp_down=mlp_down)
    small = _small_weights(p)
    big = {0: {name: arr[k].astype(BF16) for name, (arr, k) in zip(BIG_NAMES, _big_f32(p, 0))}}

    cfg_p = TileCfg(n_seq=1, seq_rows=ROW_TILE, steps_per_seq=seq // STEP_ROWS, gate_chunk=A_CHUNK,
                    emit_v=False, n_steps=batch * seq // STEP_ROWS, skewed=True)
    zb = jnp.zeros((n_even, batch * SUBLANES, D_B), F32)
    zc = jnp.zeros((n_odd, batch * SUBLANES, D_C), F32)
    y_p, tb_p, th_p, tc_p, _ = _trunk(x_prompt.reshape(batch * seq, D_MODEL), zb, zb, zc, small, big, cfg_p, p)

    cfg_s = TileCfg(n_seq=ROW_TILE // dec_seq, seq_rows=dec_seq, steps_per_seq=1, gate_chunk=dec_seq,
                    emit_v=True, n_steps=dec_batch * dec_seq // STEP_ROWS, skewed=False)
    sb = _pad_state(cache_b_conv)
    sh = _pad_state(state_b_h[:, :, None, :])
    sc = _pad_state(cache_c_conv)
    y_s, tb_s, th_s, tc_s, v_s = _trunk(x_sample.reshape(dec_batch * dec_seq, D_MODEL), sb, sh, sc, small, big,
                                         cfg_s)

    def tails(ts, n, keep):
        t = jnp.stack(ts).reshape(len(ts), n, SUBLANES, -1)
        return t[:, :, SUBLANES - keep:, :]

    return (y_p.reshape(batch, seq, D_MODEL),
            y_s.reshape(dec_batch, dec_seq, D_MODEL),
            jnp.stack(v_s).reshape(n_even, dec_batch, dec_seq, D_A),
            tails(tb_p, batch, B_CONV - 1),
            tails(th_p, batch, 1)[:, :, 0, :],
            tails(tc_p, batch, C_CONV - 1),
            tails(tb_s, dec_batch, B_CONV - 1),
            tails(th_s, dec_batch, 1)[:, :, 0, :],
            tails(tc_s, dec_batch, C_CONV - 1))
```

```python
import functools
from typing import NamedTuple

import jax
import jax.numpy as jnp
from jax import lax
from jax.experimental import pallas as pl
from jax.experimental.pallas import tpu as pltpu

D_MODEL = 1024
DEPTH = 4
CHUNK = 64
A_CHUNK = 128
D_A = D_MODEL // 2
A_HEADS = 4
A_HEAD_DIM = D_A // A_HEADS
D_B = D_MODEL // 2
B_HEADS = 8
B_HEAD_DIM = D_B // B_HEADS
B_CONV = 4
LRU_C = 8.0
D_C = D_MODEL
C_CONV = 3
D_FF = 4 * D_MODEL
EPS = 1e-6

SUBLANES = 8
V7X_VMEM_BYTES = 64 * 1024 * 1024
VMEM_LIMIT_BYTES = V7X_VMEM_BYTES - 8 * 1024 * 1024
ROW_TILE = 512
TILES_PER_STEP = 1
STEP_ROWS = ROW_TILE * TILES_PER_STEP

F32 = jnp.float32
BF16 = jnp.bfloat16


class TileCfg(NamedTuple):
    n_seq: int
    seq_rows: int
    steps_per_seq: int
    gate_chunk: int
    emit_v: bool
    n_steps: int
    skewed: bool

    @property
    def state_rows(self):
        groups = TILES_PER_STEP if self.n_seq > 1 else 1
        return groups * self.n_seq * SUBLANES

    def tile_state(self, t):
        if self.n_seq == 1:
            return slice(0, SUBLANES)
        return slice(t * self.n_seq * SUBLANES, (t + 1) * self.n_seq * SUBLANES)


def _rmsnorm(x, g):
    ms = jnp.mean(x * x, axis=-1, keepdims=True)
    return x * lax.rsqrt(ms + EPS) * g


def _shift_rows(x, hist, k, cfg):
    if k == 0:
        return x
    rolled = pltpu.roll(x, k, 0)
    row = lax.broadcasted_iota(jnp.int32, (SUBLANES, x.shape[1]), 0)
    pieces = []
    for s in range(cfg.n_seq):
        lo = s * cfg.seq_rows
        h = pltpu.roll(hist[s * SUBLANES:(s + 1) * SUBLANES], k, 0)
        pieces.append(jnp.where(row < k, h, rolled[lo:lo + SUBLANES]))
        pieces.append(rolled[lo + SUBLANES:lo + cfg.seq_rows])
    return jnp.concatenate(pieces, axis=0)


def _causal_conv(x, hist, w, cfg):
    width = w.shape[0]
    y = w[0:1] * _shift_rows(x, hist, width - 1, cfg)
    for k in range(1, width):
        y = y + w[k:k + 1] * _shift_rows(x, hist, width - 1 - k, cfg)
    return y


def _seq_tails(x, cfg):
    pieces = [x[(s + 1) * cfg.seq_rows - SUBLANES:(s + 1) * cfg.seq_rows] for s in range(cfg.n_seq)]
    return pieces[0] if len(pieces) == 1 else jnp.concatenate(pieces, axis=0)


def _linear_scan(a, b, h_prev, cfg):
    row = lax.broadcasted_iota(jnp.int32, a.shape, 0) & (SUBLANES - 1)
    d = 1
    while d < SUBLANES:
        keep = row >= d
        a_sh = pltpu.roll(a, d, 0)
        b_sh = pltpu.roll(b, d, 0)
        b = jnp.where(keep, a * b_sh + b, b)
        a = jnp.where(keep, a * a_sh, a)
        d *= 2
    groups_per_seq = cfg.seq_rows // SUBLANES
    out = []
    for s in range(cfg.n_seq):
        carry = h_prev[s * SUBLANES + SUBLANES - 1:(s + 1) * SUBLANES]
        for g in range(groups_per_seq):
            lo = s * cfg.seq_rows + g * SUBLANES
            hg = b[lo:lo + SUBLANES] + a[lo:lo + SUBLANES] * carry
            out.append(hg)
            carry = hg[SUBLANES - 1:SUBLANES]
    return jnp.concatenate(out, axis=0)


def _spatial_gating(v, ws_ref, bias_ref, cfg):
    lc = cfg.gate_chunk
    n_chunks = v.shape[0] // lc
    vb = v.astype(BF16)
    pi = lax.shift_right_logical(lax.broadcasted_iota(jnp.int32, (lc, lc), 0), 6)
    pj = lax.shift_right_logical(lax.broadcasted_iota(jnp.int32, (lc, lc), 1), 6)
    mask = pj <= pi
    heads = []
    for h in range(A_HEADS):
        lanes = slice(h * A_HEAD_DIM, (h + 1) * A_HEAD_DIM)
        w = jnp.where(mask, ws_ref[h, 0:lc, 0:lc], 0.0).astype(BF16)
        vh = jnp.concatenate([vb[c * lc:(c + 1) * lc, lanes] for c in range(n_chunks)], axis=1)
        sh = jnp.dot(w, vh, preferred_element_type=F32)
        bias = bias_ref[h, 0:lc, :]
        heads.append(jnp.concatenate(
            [sh[:, c * A_HEAD_DIM:(c + 1) * A_HEAD_DIM] + bias for c in range(n_chunks)], axis=0))
    return jnp.concatenate(heads, axis=1)


def _mixer_in(x, gpre_ref, win_ref):
    hn = _rmsnorm(x, gpre_ref[...]).astype(BF16)
    return jnp.dot(hn, win_ref[...], preferred_element_type=F32)


def _mixer_out(x, mix, wout_ref, gpost_ref):
    m = jnp.dot(mix, wout_ref[...], preferred_element_type=F32)
    return x + _rmsnorm(m, gpost_ref[...])


def _mlp_up(x, gpre_ref, up_ref):
    hn = _rmsnorm(x, gpre_ref[...]).astype(BF16)
    return jnp.dot(hn, up_ref[...], preferred_element_type=F32)


def _mlp_down(x, hid, down_ref, gpost_ref):
    act = jnp.square(jnp.maximum(hid, 0.0)).astype(BF16)
    f = jnp.dot(act, down_ref[...], preferred_element_type=F32)
    return x + _rmsnorm(f, gpost_ref[...])


def _keep_state(cfg, new, old):
    return jnp.where(pl.program_id(0) < cfg.n_steps, new, old) if cfg.skewed else new


def _even_mixer(cfg, x, z, refs, states, tile, between):
    (vgain_ref, ws_ref, bias_ref, convw_ref, convb_ref, wgate_ref, ba_ref, bx_ref, lam_ref, wout_ref,
     gpost_ref) = refs
    hist_scr, h_scr, tailx_ref, tailh_ref, v_ref = states
    srows = cfg.tile_state(tile)
    u = jax.nn.gelu(z[:, 0:D_A])
    v = _rmsnorm(jax.nn.gelu(z[:, D_A:2 * D_A]), vgain_ref[...])
    xb = z[:, 2 * D_A:2 * D_A + D_B]
    gb = z[:, 2 * D_A + D_B:]
    if v_ref is not None:
        v_ref[tile * ROW_TILE:(tile + 1) * ROW_TILE, :] = v
    hist = hist_scr[srows, :]
    xc = _causal_conv(xb, hist, convw_ref[...], cfg) + convb_ref[...]
    new_hist = _keep_state(cfg, _seq_tails(xb, cfg), hist)
    hist_scr[srows, :] = new_hist
    tailx_ref[srows, :] = new_hist

    gate = _spatial_gating(v, ws_ref, bias_ref, cfg)
    half = D_B // 2
    xcb = xc.astype(BF16)
    g0 = jnp.dot(xcb[:, :half], wgate_ref[0], preferred_element_type=F32)
    g1 = jnp.dot(xcb[:, half:], wgate_ref[1], preferred_element_type=F32)
    between()

    a_out = u * gate
    r = jax.nn.sigmoid(jnp.concatenate([g0[:, :half], g1[:, :half]], axis=1) + ba_ref[...])
    ig = jax.nn.sigmoid(jnp.concatenate([g0[:, half:], g1[:, half:]], axis=1) + bx_ref[...])
    nl = -lam_ref[...]
    softplus = jnp.maximum(nl, 0.0) + jnp.log1p(jnp.exp(-jnp.abs(nl)))
    log_a = -LRU_C * r * softplus
    a = jnp.exp(log_a)
    bterm = jnp.sqrt(-jnp.tanh(log_a) * (a * a + 1.0)) * ig * xc
    h_prev = h_scr[srows, :]
    hseq = _linear_scan(a, bterm, h_prev, cfg)
    new_h = _keep_state(cfg, _seq_tails(hseq, cfg), h_prev)
    h_scr[srows, :] = new_h
    tailh_ref[srows, :] = new_h

    b_out = hseq * jax.nn.gelu(gb)
    mix = jnp.concatenate([a_out, b_out], axis=1).astype(BF16)
    return _mixer_out(x, mix, wout_ref, gpost_ref)


def _odd_mixer(cfg, x, z, refs, states, tile, between):
    convw_ref, wout_ref, gpost_ref = refs
    hist_scr, tail_ref = states
    srows = cfg.tile_state(tile)
    between()
    bg = z[:, 0:D_C]
    p = z[:, D_C:2 * D_C] * z[:, 2 * D_C:]
    hist = hist_scr[srows, :]
    y = _causal_conv(p, hist, convw_ref[...], cfg)
    new_hist = _keep_state(cfg, _seq_tails(p, cfg), hist)
    hist_scr[srows, :] = new_hist
    tail_ref[srows, :] = new_hist
    return _mixer_out(x, (bg * y).astype(BF16), wout_ref, gpost_ref)


def _paired(mixer, x, z_of, mlp_refs, mlp_x, write_mlp):
    gfpre_ref, up_ref, down_ref, gfpost_ref = mlp_refs
    z = z_of(x) if x is not None else None
    hid = _mlp_up(mlp_x, gfpre_ref, up_ref) if mlp_x is not None else None

    def mlp_down():
        if mlp_x is not None:
            write_mlp(_mlp_down(mlp_x, hid, down_ref, gfpost_ref))

    if x is None:
        mlp_down()
        return None
    return mixer(x, z, mlp_down)


def _layer_body(cfg, mixer, z_of, mlp_refs, x_ref, xo_ref, x1_scr):
    rows = lambda t: slice(t * ROW_TILE, (t + 1) * ROW_TILE)

    def write_out(t):
        def write(val):
            xo_ref[rows(t), :] = val
        return write

    if not cfg.skewed:
        x1_prev = None
        for t in range(TILES_PER_STEP + 1):
            x = x_ref[rows(t), :] if t < TILES_PER_STEP else None
            x1 = _paired(functools.partial(mixer, t), x, z_of, mlp_refs, x1_prev,
                         write_out(t - 1) if t > 0 else None)
            x1_prev = x1
        return

    @pl.when(pl.program_id(0) == 0)
    def _():
        x1_scr[...] = jnp.zeros_like(x1_scr)

    for t in range(TILES_PER_STEP):
        x1_scr[rows(t), :] = _paired(functools.partial(mixer, t), x_ref[rows(t), :], z_of, mlp_refs,
                                     x1_scr[rows(t), :], write_out(t))


def _load_states_at_sequence_start(cfg, pairs):
    i = pl.program_id(0)

    @pl.when(jnp.logical_and(lax.rem(i, cfg.steps_per_seq) == 0, i < cfg.n_steps))
    def _():
        for scr, src in pairs:
            scr[...] = src[...]


def _cast_next_layer_weights(cast_in_refs, cast_out_refs):
    for src, dst in zip(cast_in_refs, cast_out_refs):
        dst[...] = src[...].astype(BF16)


BIG_NAMES = ("win", "wout", "up", "down")

ROW_GPRE, ROW_GPOST, ROW_GFPRE, ROW_GFPOST, ROW_CCONV = 0, 1, 2, 3, 4
N_VEC_ROWS = ROW_CCONV + C_CONV
ROW_VGAIN, ROW_CONVB, ROW_BA, ROW_BX, ROW_LAM, ROW_BCONV = 0, 1, 2, 3, 4, 5
N_EVEN_VEC_ROWS = ROW_BCONV + B_CONV


def _row(ref, r, n=1):
    return ref.at[r:r + n, :]


def _even_layer_kernel(cfg, n_cast, *refs):
    refs = list(refs)
    x_ref, vec_ref, evec_ref, wsb_ref, wgate_ref, win_ref, wout_ref, up_ref, down_ref, state_in_ref = refs[:10]
    cast_in_refs = refs[10:10 + n_cast]
    del refs[:10 + n_cast]
    xo_ref, tails_ref = refs[:2]
    del refs[:2]
    v_ref = refs.pop(0) if cfg.emit_v else None
    gpre_ref, gpost_ref = _row(vec_ref, ROW_GPRE), _row(vec_ref, ROW_GPOST)
    gfpre_ref, gfpost_ref = _row(vec_ref, ROW_GFPRE), _row(vec_ref, ROW_GFPOST)
    vgain_ref, convb_ref = _row(evec_ref, ROW_VGAIN), _row(evec_ref, ROW_CONVB)
    ba_ref, bx_ref, lam_ref = _row(evec_ref, ROW_BA), _row(evec_ref, ROW_BX), _row(evec_ref, ROW_LAM)
    convw_ref = _row(evec_ref, ROW_BCONV, B_CONV)
    ws_ref, bias_ref = wsb_ref.at[0], wsb_ref.at[1]
    hist_in_ref, h_in_ref = state_in_ref.at[:, 0:D_B], state_in_ref.at[:, D_B:2 * D_B]
    tailx_ref, tailh_ref = tails_ref.at[:, 0:D_B], tails_ref.at[:, D_B:2 * D_B]
    cast_out_refs = refs[:n_cast]
    del refs[:n_cast]
    hist_scr, h_scr = refs[:2]
    x1_scr = refs[2] if cfg.skewed else None

    mixer_refs = (vgain_ref, ws_ref, bias_ref, convw_ref, convb_ref, wgate_ref, ba_ref, bx_ref, lam_ref,
                  wout_ref, gpost_ref)
    states = (hist_scr, h_scr, tailx_ref, tailh_ref, v_ref)
    _load_states_at_sequence_start(cfg, [(hist_scr, hist_in_ref), (h_scr, h_in_ref)])

    def mixer(tile, x, z, between):
        return _even_mixer(cfg, x, z, mixer_refs, states, tile, between)

    _layer_body(cfg, mixer, lambda x: _mixer_in(x, gpre_ref, win_ref), (gfpre_ref, up_ref, down_ref, gfpost_ref),
                x_ref, xo_ref, x1_scr)
    _cast_next_layer_weights(cast_in_refs, cast_out_refs)


def _odd_layer_kernel(cfg, n_cast, *refs):
    refs = list(refs)
    x_ref, vec_ref, win_ref, wout_ref, up_ref, down_ref, hist_in_ref = refs[:7]
    cast_in_refs = refs[7:7 + n_cast]
    del refs[:7 + n_cast]
    gpre_ref, gpost_ref = _row(vec_ref, ROW_GPRE), _row(vec_ref, ROW_GPOST)
    gfpre_ref, gfpost_ref = _row(vec_ref, ROW_GFPRE), _row(vec_ref, ROW_GFPOST)
    convw_ref = _row(vec_ref, ROW_CCONV, C_CONV)
    xo_ref, tail_ref = refs[:2]
    del refs[:2]
    cast_out_refs = refs[:n_cast]
    del refs[:n_cast]
    hist_scr = refs[0]
    x1_scr = refs[1] if cfg.skewed else None

    _load_states_at_sequence_start(cfg, [(hist_scr, hist_in_ref)])

    def mixer(tile, x, z, between):
        return _odd_mixer(cfg, x, z, (convw_ref, wout_ref, gpost_ref), (hist_scr, tail_ref), tile, between)

    _layer_body(cfg, mixer, lambda x: _mixer_in(x, gpre_ref, win_ref), (gfpre_ref, up_ref, down_ref, gfpost_ref),
                x_ref, xo_ref, x1_scr)
    _cast_next_layer_weights(cast_in_refs, cast_out_refs)


def _resident(arr, layer):
    if layer is None:
        zeros = (0,) * arr.ndim
        return pl.BlockSpec(arr.shape, lambda i: zeros, pipeline_mode=pl.Buffered(1))
    zeros = (0,) * (arr.ndim - 1)
    return pl.BlockSpec((None,) + arr.shape[1:], lambda i: (layer,) + zeros, pipeline_mode=pl.Buffered(1))


def _cast_specs(cfg, cast_src):
    in_specs, out_specs, out_shape = [], [], []
    for arr, k in cast_src:
        _, n_rows, n_cols = arr.shape
        chunk = n_rows // cfg.n_steps
        in_specs.append(pl.BlockSpec((None, chunk, n_cols), lambda i, _k=k: (_k, _mixer_step(cfg, i), 0)))
        out_specs.append(pl.BlockSpec((chunk, n_cols), lambda i: (_mixer_step(cfg, i), 0)))
        out_shape.append(jax.ShapeDtypeStruct((n_rows, n_cols), BF16))
    return in_specs, out_specs, out_shape


def _mixer_step(cfg, i):
    return jnp.minimum(i, cfg.n_steps - 1)


def _state_in_spec(cfg, width, k):
    return pl.BlockSpec((None, cfg.state_rows, width),
                        lambda i: (k, _mixer_step(cfg, i) // cfg.steps_per_seq, 0))


def _state_spec(cfg, width):
    return pl.BlockSpec((cfg.state_rows, width), lambda i: (_mixer_step(cfg, i) // cfg.steps_per_seq, 0))


def _row_in_spec(cfg, width):
    return pl.BlockSpec((STEP_ROWS, width), lambda i: (_mixer_step(cfg, i), 0))


def _row_out_spec(cfg, width):
    if cfg.skewed:
        return pl.BlockSpec((STEP_ROWS, width), lambda i: (jnp.maximum(i - 1, 0), 0))
    return pl.BlockSpec((STEP_ROWS, width), lambda i: (i, 0))


def _grid(cfg):
    return (cfg.n_steps + 1,) if cfg.skewed else (cfg.n_steps,)


def _skew_scratch(cfg):
    return [pltpu.VMEM((STEP_ROWS, D_MODEL), F32)] if cfg.skewed else []


def _compiler_params():
    return pltpu.CompilerParams(dimension_semantics=("arbitrary",), vmem_limit_bytes=VMEM_LIMIT_BYTES)


def _even_layer(cfg, x, state, w, big, layer, cast_src=()):
    rows = x.shape[0]
    e = layer // 2
    consts = [(w["vec"], layer), (w["evec"], e), (w["wsb"], e), (w["wgate"], e)] + [(big[n], None) for n in BIG_NAMES]
    cast_in, cast_out, cast_shape = _cast_specs(cfg, cast_src)
    in_specs = ([_row_in_spec(cfg, D_MODEL)] + [_resident(c, k) for c, k in consts]
                + [_state_in_spec(cfg, 2 * D_B, e)] + cast_in)
    out_shape = [jax.ShapeDtypeStruct((rows, D_MODEL), F32),
                 jax.ShapeDtypeStruct((state.shape[1], 2 * D_B), F32)]
    out_specs = [_row_out_spec(cfg, D_MODEL), _state_spec(cfg, 2 * D_B)]
    if cfg.emit_v:
        out_shape.append(jax.ShapeDtypeStruct((rows, D_A), F32))
        out_specs.append(_row_in_spec(cfg, D_A))
    scratch = ([pltpu.VMEM((cfg.state_rows, D_B), F32), pltpu.VMEM((cfg.state_rows, D_B), F32)]
               + _skew_scratch(cfg))
    return pl.pallas_call(
        functools.partial(_even_layer_kernel, cfg, len(cast_src)),
        grid=_grid(cfg),
        in_specs=in_specs,
        out_specs=out_specs + cast_out,
        out_shape=out_shape + cast_shape,
        scratch_shapes=scratch,
        compiler_params=_compiler_params(),
        name="even_layer",
    )(x, *[c for c, _ in consts], state, *[arr for arr, _ in cast_src])


def _odd_layer(cfg, x, hist, w, big, layer, cast_src=()):
    rows = x.shape[0]
    o = layer // 2
    consts = [(w["vec"], layer)] + [(big[n], None) for n in BIG_NAMES]
    cast_in, cast_out, cast_shape = _cast_specs(cfg, cast_src)
    in_specs = ([_row_in_spec(cfg, D_MODEL)] + [_resident(c, k) for c, k in consts]
                + [_state_in_spec(cfg, D_C, o)] + cast_in)
    out_shape = [jax.ShapeDtypeStruct((rows, D_MODEL), F32),
                 jax.ShapeDtypeStruct((hist.shape[1], D_C), F32)]
    out_specs = [_row_out_spec(cfg, D_MODEL), _state_spec(cfg, D_C)]
    scratch = [pltpu.VMEM((cfg.state_rows, D_C), F32)] + _skew_scratch(cfg)
    return pl.pallas_call(
        functools.partial(_odd_layer_kernel, cfg, len(cast_src)),
        grid=_grid(cfg),
        in_specs=in_specs,
        out_specs=out_specs + cast_out,
        out_shape=out_shape + cast_shape,
        scratch_shapes=scratch,
        compiler_params=_compiler_params(),
        name="odd_layer",
    )(x, *[c for c, _ in consts], hist, *[arr for arr, _ in cast_src])


def _block_diag_gates(wa, wx):
    heads_per_half = B_HEADS // 2
    eye = jnp.eye(heads_per_half, dtype=wa.dtype)

    def bd(w):
        return jnp.einsum("eqhij,hg->eqhigj", w, eye).reshape(w.shape[0], 2, D_B // 2, D_B // 2)

    split = lambda w: w.reshape(w.shape[0], 2, heads_per_half, B_HEAD_DIM, B_HEAD_DIM)
    return jnp.concatenate([bd(split(wa)), bd(split(wx))], axis=-1).astype(BF16)


def _pad_state(rows):
    layers, n, r, c = rows.shape
    return jnp.pad(rows, ((0, 0), (0, 0), (SUBLANES - r, 0), (0, 0))).reshape(layers, n * SUBLANES, c)


def _small_weights(p):
    depth = p["norm_mix_pre"].shape[0]
    gains = jnp.stack([p["norm_mix_pre"], p["norm_mix_post"], p["norm_ffn_pre"], p["norm_ffn_post"]], axis=1)
    cconv = jnp.zeros((depth, C_CONV, D_C), F32).at[1::2].set(p["c_conv_w"])
    row = lambda v: v[:, None, :]
    evec = jnp.concatenate([row(p["a_v_gain"]), row(p["b_conv_b"]), row(p["b_ba"]), row(p["b_bx"]),
                            row(p["b_lambda"]), p["b_conv_w"]], axis=1)
    a_b_s = p["a_b_s"]
    bias = jnp.broadcast_to(a_b_s[..., None], a_b_s.shape + (A_HEAD_DIM,))
    return dict(vec=jnp.concatenate([gains, cconv], axis=1), evec=evec, wsb=jnp.stack([p["a_w_s"], bias], axis=1),
                wgate=_block_diag_gates(p["b_wa"], p["b_wx"]))


def _big_f32(p, layer):
    k = layer // 2
    if layer % 2 == 0:
        return [(p["w_in_even"], k), (p["w_out_even"], k), (p["mlp_up"], layer), (p["mlp_down"], layer)]
    return [(p["c_w_in"], k), (p["c_w_out"], k), (p["mlp_up"], layer), (p["mlp_down"], layer)]


def _trunk(x, b_state, c_conv, small, big, cfg, p=None):
    tails_b, tails_h, tails_c, v_rows = [], [], [], []
    for l in range(DEPTH):
        cast_src = _big_f32(p, l + 1) if p is not None and l + 1 < DEPTH else ()
        if l % 2 == 0:
            outs = _even_layer(cfg, x, b_state, small, big[l], l, cast_src)
            n_fixed = 3 if cfg.emit_v else 2
            tails_b.append(outs[1][:, :D_B])
            tails_h.append(outs[1][:, D_B:])
            if cfg.emit_v:
                v_rows.append(outs[2])
        else:
            outs = _odd_layer(cfg, x, c_conv, small, big[l], l, cast_src)
            n_fixed = 2
            tails_c.append(outs[1])
        x = outs[0]
        if cast_src:
            big[l + 1] = dict(zip(BIG_NAMES, outs[n_fixed:]))
    return x, tails_b, tails_h, tails_c, v_rows


def kernel(x_prompt, x_sample, cache_b_conv, state_b_h, cache_c_conv, norm_mix_pre, norm_mix_post, norm_ffn_pre, norm_ffn_post, w_in_even, a_v_gain, a_w_s, a_b_s, b_conv_w, b_conv_b, b_wa, b_ba, b_wx, b_bx, b_lambda, w_out_even, c_w_in, c_conv_w, c_w_out, mlp_up, mlp_down):
    batch, seq, _ = x_prompt.shape
    dec_batch, dec_seq, _ = x_sample.shape
    n_even, n_odd = w_in_even.shape[0], c_w_in.shape[0]
    p = dict(norm_mix_pre=norm_mix_pre, norm_mix_post=norm_mix_post, norm_ffn_pre=norm_ffn_pre,
             norm_ffn_post=norm_ffn_post, w_in_even=w_in_even, a_v_gain=a_v_gain, a_w_s=a_w_s, a_b_s=a_b_s,
             b_conv_w=b_conv_w, b_conv_b=b_conv_b, b_wa=b_wa, b_wx=b_wx, b_ba=b_ba, b_bx=b_bx,
             b_lambda=b_lambda, w_out_even=w_out_even, c_w_in=c_w_in, c_conv_w=c_conv_w, c_w_out=c_w_out,
             mlp_up=mlp_up, mlp_down=mlp_down)
    small = _small_weights(p)
    big = {0: {name: arr[k].astype(BF16) for name, (arr, k) in zip(BIG_NAMES, _big_f32(p, 0))}}

    cfg_p = TileCfg(n_seq=1, seq_rows=ROW_TILE, steps_per_seq=seq // STEP_ROWS, gate_chunk=A_CHUNK,
                    emit_v=False, n_steps=batch * seq // STEP_ROWS, skewed=True)
    zb = jnp.zeros((n_even, batch * SUBLANES, 2 * D_B), F32)
    zc = jnp.zeros((n_odd, batch * SUBLANES, D_C), F32)
    y_p, tb_p, th_p, tc_p, _ = _trunk(x_prompt.reshape(batch * seq, D_MODEL), zb, zc, small, big, cfg_p, p)

    cfg_s = TileCfg(n_seq=ROW_TILE // dec_seq, seq_rows=dec_seq, steps_per_seq=1, gate_chunk=dec_seq,
                    emit_v=True, n_steps=dec_batch * dec_seq // STEP_ROWS, skewed=False)
    sb = jnp.concatenate([_pad_state(cache_b_conv), _pad_state(state_b_h[:, :, None, :])], axis=-1)
    sc = _pad_state(cache_c_conv)
    y_s, tb_s, th_s, tc_s, v_s = _trunk(x_sample.reshape(dec_batch * dec_seq, D_MODEL), sb, sc, small, big, cfg_s)

    def tails(ts, n, keep):
        t = jnp.stack(ts).reshape(len(ts), n, SUBLANES, -1)
        return t[:, :, SUBLANES - keep:, :]

    return (y_p.reshape(batch, seq, D_MODEL),
            y_s.reshape(dec_batch, dec_seq, D_MODEL),
            jnp.stack(v_s).reshape(n_even, dec_batch, dec_seq, D_A),
            tails(tb_p, batch, B_CONV - 1),
            tails(th_p, batch, 1)[:, :, 0, :],
            tails(tc_p, batch, C_CONV - 1),
            tails(tb_s, dec_batch, B_CONV - 1),
            tails(th_s, dec_batch, 1)[:, :, 0, :],
            tails(tc_s, dec_batch, C_CONV - 1))
```

```python
import functools
from typing import NamedTuple

import jax
import jax.numpy as jnp
from jax import lax
from jax.experimental import pallas as pl
from jax.experimental.pallas import tpu as pltpu

D_MODEL = 1024
DEPTH = 4
CHUNK = 64
A_CHUNK = 128
D_A = D_MODEL // 2
A_HEADS = 4
A_HEAD_DIM = D_A // A_HEADS
D_B = D_MODEL // 2
B_HEADS = 8
B_HEAD_DIM = D_B // B_HEADS
B_CONV = 4
LRU_C = 8.0
D_C = D_MODEL
C_CONV = 3
D_FF = 4 * D_MODEL
EPS = 1e-6

SUBLANES = 8
LANES = 128
V7X_VMEM_BYTES = 64 * 1024 * 1024
VMEM_LIMIT_BYTES = V7X_VMEM_BYTES - 8 * 1024 * 1024
ROW_TILE = 512
TILES_PER_STEP = 1
STEP_ROWS = ROW_TILE * TILES_PER_STEP

F32 = jnp.float32
BF16 = jnp.bfloat16


class TileCfg(NamedTuple):
    n_seq: int
    seq_rows: int
    steps_per_seq: int
    gate_chunk: int
    emit_v: bool
    n_steps: int
    skewed: bool

    @property
    def state_rows(self):
        groups = TILES_PER_STEP if self.n_seq > 1 else 1
        return groups * self.n_seq * SUBLANES

    def tile_state(self, t):
        if self.n_seq == 1:
            return slice(0, SUBLANES)
        return slice(t * self.n_seq * SUBLANES, (t + 1) * self.n_seq * SUBLANES)


def _rmsnorm(x, g):
    ms = jnp.mean(x * x, axis=-1, keepdims=True)
    return x * lax.rsqrt(ms + EPS) * g


def _shift_rows(x, hist, k, cfg):
    if k == 0:
        return x
    rolled = pltpu.roll(x, k, 0)
    row = lax.broadcasted_iota(jnp.int32, (SUBLANES, x.shape[1]), 0)
    pieces = []
    for s in range(cfg.n_seq):
        lo = s * cfg.seq_rows
        h = pltpu.roll(hist[s * SUBLANES:(s + 1) * SUBLANES], k, 0)
        pieces.append(jnp.where(row < k, h, rolled[lo:lo + SUBLANES]))
        pieces.append(rolled[lo + SUBLANES:lo + cfg.seq_rows])
    return jnp.concatenate(pieces, axis=0)


def _causal_conv(x, hist, w, cfg):
    width = w.shape[0]
    y = w[0:1] * _shift_rows(x, hist, width - 1, cfg)
    for k in range(1, width):
        y = y + w[k:k + 1] * _shift_rows(x, hist, width - 1 - k, cfg)
    return y


def _seq_tails(x, cfg):
    pieces = [x[(s + 1) * cfg.seq_rows - SUBLANES:(s + 1) * cfg.seq_rows] for s in range(cfg.n_seq)]
    return pieces[0] if len(pieces) == 1 else jnp.concatenate(pieces, axis=0)


def _linear_scan(a, b, h_prev, cfg):
    row = lax.broadcasted_iota(jnp.int32, a.shape, 0) & (SUBLANES - 1)
    d = 1
    while d < SUBLANES:
        keep = row >= d
        a_sh = pltpu.roll(a, d, 0)
        b_sh = pltpu.roll(b, d, 0)
        b = jnp.where(keep, a * b_sh + b, b)
        a = jnp.where(keep, a * a_sh, a)
        d *= 2
    groups_per_seq = cfg.seq_rows // SUBLANES
    out = []
    for s in range(cfg.n_seq):
        carry = h_prev[s * SUBLANES + SUBLANES - 1:(s + 1) * SUBLANES]
        for g in range(groups_per_seq):
            lo = s * cfg.seq_rows + g * SUBLANES
            hg = b[lo:lo + SUBLANES] + a[lo:lo + SUBLANES] * carry
            out.append(hg)
            carry = hg[SUBLANES - 1:SUBLANES]
    return jnp.concatenate(out, axis=0)


def _spatial_gating(v, ws_ref, bias_ref, cfg):
    lc = cfg.gate_chunk
    n_chunks = v.shape[0] // lc
    vb = v.astype(BF16)
    pi = lax.shift_right_logical(lax.broadcasted_iota(jnp.int32, (lc, lc), 0), 6)
    pj = lax.shift_right_logical(lax.broadcasted_iota(jnp.int32, (lc, lc), 1), 6)
    mask = pj <= pi
    heads = []
    for h in range(A_HEADS):
        lanes = slice(h * A_HEAD_DIM, (h + 1) * A_HEAD_DIM)
        w = jnp.where(mask, ws_ref[h, 0:lc, 0:lc], 0.0).astype(BF16)
        vh = jnp.concatenate([vb[c * lc:(c + 1) * lc, lanes] for c in range(n_chunks)], axis=1)
        sh = jnp.dot(w, vh, preferred_element_type=F32)
        bias = bias_ref[h, 0:lc, :]
        heads.append(jnp.concatenate(
            [sh[:, c * A_HEAD_DIM:(c + 1) * A_HEAD_DIM] + bias for c in range(n_chunks)], axis=0))
    return jnp.concatenate(heads, axis=1)


def _mixer_in(x, gpre_ref, win_ref):
    hn = _rmsnorm(x, gpre_ref[...]).astype(BF16)
    return jnp.dot(hn, win_ref[...], preferred_element_type=F32)


def _mixer_out(x, mix, wout_ref, gpost_ref):
    m = jnp.dot(mix, wout_ref[...], preferred_element_type=F32)
    return x + _rmsnorm(m, gpost_ref[...])


def _mlp_up(x, gpre_ref, up_ref):
    hn = _rmsnorm(x, gpre_ref[...]).astype(BF16)
    return jnp.dot(hn, up_ref[...], preferred_element_type=F32)


def _mlp_down(x, hid, down_ref, gpost_ref):
    act = jnp.square(jnp.maximum(hid, 0.0)).astype(BF16)
    f = jnp.dot(act, down_ref[...], preferred_element_type=F32)
    return x + _rmsnorm(f, gpost_ref[...])


def _keep_state(cfg, new, old):
    return jnp.where(pl.program_id(0) < cfg.n_steps, new, old) if cfg.skewed else new


def _even_mixer(cfg, x, z, refs, states, tile, between):
    (vgain_ref, ws_ref, bias_ref, convw_ref, convb_ref, wgate_ref, ba_ref, bx_ref, lam_ref, wout_ref,
     gpost_ref) = refs
    hist_scr, h_scr, tailx_ref, tailh_ref, v_ref = states
    srows = cfg.tile_state(tile)
    u = jax.nn.gelu(z[:, 0:D_A])
    v = _rmsnorm(jax.nn.gelu(z[:, D_A:2 * D_A]), vgain_ref[...])
    xb = z[:, 2 * D_A:2 * D_A + D_B]
    gb = z[:, 2 * D_A + D_B:]
    if v_ref is not None:
        v_ref[tile * ROW_TILE:(tile + 1) * ROW_TILE, :] = v
    hist = hist_scr[srows, :]
    xc = _causal_conv(xb, hist, convw_ref[...], cfg) + convb_ref[...]
    new_hist = _keep_state(cfg, _seq_tails(xb, cfg), hist)
    hist_scr[srows, :] = new_hist
    tailx_ref[srows, :] = new_hist

    gate = _spatial_gating(v, ws_ref, bias_ref, cfg)
    half = D_B // 2
    xcb = xc.astype(BF16)
    g0 = jnp.dot(xcb[:, :half], wgate_ref[0], preferred_element_type=F32)
    g1 = jnp.dot(xcb[:, half:], wgate_ref[1], preferred_element_type=F32)
    between()

    a_out = u * gate
    r = jax.nn.sigmoid(jnp.concatenate([g0[:, :half], g1[:, :half]], axis=1) + ba_ref[...])
    ig = jax.nn.sigmoid(jnp.concatenate([g0[:, half:], g1[:, half:]], axis=1) + bx_ref[...])
    nl = -lam_ref[...]
    softplus = jnp.maximum(nl, 0.0) + jnp.log1p(jnp.exp(-jnp.abs(nl)))
    log_a = -LRU_C * r * softplus
    a = jnp.exp(log_a)
    bterm = jnp.sqrt(-jnp.tanh(log_a) * (a * a + 1.0)) * ig * xc
    h_prev = h_scr[srows, :]
    hseq = _linear_scan(a, bterm, h_prev, cfg)
    new_h = _keep_state(cfg, _seq_tails(hseq, cfg), h_prev)
    h_scr[srows, :] = new_h
    tailh_ref[srows, :] = new_h

    b_out = hseq * jax.nn.gelu(gb)
    mix = jnp.concatenate([a_out, b_out], axis=1).astype(BF16)
    return _mixer_out(x, mix, wout_ref, gpost_ref)


def _odd_mixer(cfg, x, z, refs, states, tile, between):
    convw_ref, wout_ref, gpost_ref = refs
    hist_scr, tail_ref = states
    srows = cfg.tile_state(tile)
    between()
    bg = z[:, 0:D_C]
    p = z[:, D_C:2 * D_C] * z[:, 2 * D_C:]
    hist = hist_scr[srows, :]
    y = _causal_conv(p, hist, convw_ref[...], cfg)
    new_hist = _keep_state(cfg, _seq_tails(p, cfg), hist)
    hist_scr[srows, :] = new_hist
    tail_ref[srows, :] = new_hist
    return _mixer_out(x, (bg * y).astype(BF16), wout_ref, gpost_ref)


def _paired(mixer, x, z_of, mlp_refs, mlp_x, write_mlp):
    gfpre_ref, up_ref, down_ref, gfpost_ref = mlp_refs
    z = z_of(x) if x is not None else None
    hid = _mlp_up(mlp_x, gfpre_ref, up_ref) if mlp_x is not None else None

    def mlp_down():
        if mlp_x is not None:
            write_mlp(_mlp_down(mlp_x, hid, down_ref, gfpost_ref))

    if x is None:
        mlp_down()
        return None
    return mixer(x, z, mlp_down)


def _layer_body(cfg, mixer, z_of, mlp_refs, x_ref, xo_ref, x1_scr):
    rows = lambda t: slice(t * ROW_TILE, (t + 1) * ROW_TILE)

    def write_out(t):
        def write(val):
            xo_ref[rows(t), :] = val
        return write

    if not cfg.skewed:
        x1_prev = None
        for t in range(TILES_PER_STEP + 1):
            x = x_ref[rows(t), :] if t < TILES_PER_STEP else None
            x1 = _paired(functools.partial(mixer, t), x, z_of, mlp_refs, x1_prev,
                         write_out(t - 1) if t > 0 else None)
            x1_prev = x1
        return

    @pl.when(pl.program_id(0) == 0)
    def _():
        x1_scr[...] = jnp.zeros_like(x1_scr)

    for t in range(TILES_PER_STEP):
        x1_scr[rows(t), :] = _paired(functools.partial(mixer, t), x_ref[rows(t), :], z_of, mlp_refs,
                                     x1_scr[rows(t), :], write_out(t))


def _load_states_at_sequence_start(cfg, pairs):
    i = pl.program_id(0)

    @pl.when(jnp.logical_and(lax.rem(i, cfg.steps_per_seq) == 0, i < cfg.n_steps))
    def _():
        for scr, src in pairs:
            scr[...] = src[...]


def _zero_computed_from(tile):
    bits = pltpu.bitcast(tile, jnp.uint32)
    sixteen = jnp.uint32(16)
    return pltpu.bitcast(lax.shift_right_logical(lax.shift_right_logical(bits, sixteen), sixteen), F32)


def _cast_next_layer_weights(cast_in_refs, cast_out_refs, after):
    zero = _zero_computed_from(after)
    for src, dst in zip(cast_in_refs, cast_out_refs):
        rows, cols = src.shape
        dst[...] = (src[...] + jnp.tile(zero, (rows // SUBLANES, cols // LANES))).astype(BF16)


BIG_NAMES = ("win", "wout", "up", "down")

ROW_GPRE, ROW_GPOST, ROW_GFPRE, ROW_GFPOST, ROW_CCONV = 0, 1, 2, 3, 4
N_VEC_ROWS = ROW_CCONV + C_CONV
ROW_VGAIN, ROW_CONVB, ROW_BA, ROW_BX, ROW_LAM, ROW_BCONV = 0, 1, 2, 3, 4, 5
N_EVEN_VEC_ROWS = ROW_BCONV + B_CONV


def _row(ref, r, n=1):
    return ref.at[r:r + n, :]


def _even_layer_kernel(cfg, n_cast, *refs):
    refs = list(refs)
    x_ref, vec_ref, evec_ref, wsb_ref, wgate_ref, win_ref, wout_ref, up_ref, down_ref, state_in_ref = refs[:10]
    cast_in_refs = refs[10:10 + n_cast]
    del refs[:10 + n_cast]
    xo_ref, tails_ref = refs[:2]
    del refs[:2]
    v_ref = refs.pop(0) if cfg.emit_v else None
    gpre_ref, gpost_ref = _row(vec_ref, ROW_GPRE), _row(vec_ref, ROW_GPOST)
    gfpre_ref, gfpost_ref = _row(vec_ref, ROW_GFPRE), _row(vec_ref, ROW_GFPOST)
    vgain_ref, convb_ref = _row(evec_ref, ROW_VGAIN), _row(evec_ref, ROW_CONVB)
    ba_ref, bx_ref, lam_ref = _row(evec_ref, ROW_BA), _row(evec_ref, ROW_BX), _row(evec_ref, ROW_LAM)
    convw_ref = _row(evec_ref, ROW_BCONV, B_CONV)
    ws_ref, bias_ref = wsb_ref.at[0], wsb_ref.at[1]
    hist_in_ref, h_in_ref = state_in_ref.at[:, 0:D_B], state_in_ref.at[:, D_B:2 * D_B]
    tailx_ref, tailh_ref = tails_ref.at[:, 0:D_B], tails_ref.at[:, D_B:2 * D_B]
    cast_out_refs = refs[:n_cast]
    del refs[:n_cast]
    hist_scr, h_scr = refs[:2]
    x1_scr = refs[2] if cfg.skewed else None

    mixer_refs = (vgain_ref, ws_ref, bias_ref, convw_ref, convb_ref, wgate_ref, ba_ref, bx_ref, lam_ref,
                  wout_ref, gpost_ref)
    states = (hist_scr, h_scr, tailx_ref, tailh_ref, v_ref)
    _load_states_at_sequence_start(cfg, [(hist_scr, hist_in_ref), (h_scr, h_in_ref)])

    def mixer(tile, x, z, between):
        return _even_mixer(cfg, x, z, mixer_refs, states, tile, between)

    def mixer_in(x):
        z = _mixer_in(x, gpre_ref, win_ref)
        _cast_next_layer_weights(cast_in_refs, cast_out_refs, after=z[0:SUBLANES, 0:LANES])
        return z

    _layer_body(cfg, mixer, mixer_in, (gfpre_ref, up_ref, down_ref, gfpost_ref), x_ref, xo_ref, x1_scr)


def _odd_layer_kernel(cfg, n_cast, *refs):
    refs = list(refs)
    x_ref, vec_ref, win_ref, wout_ref, up_ref, down_ref, hist_in_ref = refs[:7]
    cast_in_refs = refs[7:7 + n_cast]
    del refs[:7 + n_cast]
    gpre_ref, gpost_ref = _row(vec_ref, ROW_GPRE), _row(vec_ref, ROW_GPOST)
    gfpre_ref, gfpost_ref = _row(vec_ref, ROW_GFPRE), _row(vec_ref, ROW_GFPOST)
    convw_ref = _row(vec_ref, ROW_CCONV, C_CONV)
    xo_ref, tail_ref = refs[:2]
    del refs[:2]
    cast_out_refs = refs[:n_cast]
    del refs[:n_cast]
    hist_scr = refs[0]
    x1_scr = refs[1] if cfg.skewed else None

    _load_states_at_sequence_start(cfg, [(hist_scr, hist_in_ref)])

    def mixer(tile, x, z, between):
        return _odd_mixer(cfg, x, z, (convw_ref, wout_ref, gpost_ref), (hist_scr, tail_ref), tile, between)

    def mixer_in(x):
        z = _mixer_in(x, gpre_ref, win_ref)
        _cast_next_layer_weights(cast_in_refs, cast_out_refs, after=z[0:SUBLANES, 0:LANES])
        return z

    _layer_body(cfg, mixer, mixer_in, (gfpre_ref, up_ref, down_ref, gfpost_ref), x_ref, xo_ref, x1_scr)


def _resident(arr, layer):
    if layer is None:
        zeros = (0,) * arr.ndim
        return pl.BlockSpec(arr.shape, lambda i: zeros, pipeline_mode=pl.Buffered(1))
    zeros = (0,) * (arr.ndim - 1)
    return pl.BlockSpec((None,) + arr.shape[1:], lambda i: (layer,) + zeros, pipeline_mode=pl.Buffered(1))


def _cast_specs(cfg, cast_src):
    in_specs, out_specs, out_shape = [], [], []
    for arr, k in cast_src:
        _, n_rows, n_cols = arr.shape
        chunk = n_rows // cfg.n_steps
        in_specs.append(pl.BlockSpec((None, chunk, n_cols), lambda i, _k=k: (_k, _mixer_step(cfg, i), 0)))
        out_specs.append(pl.BlockSpec((chunk, n_cols), lambda i: (_mixer_step(cfg, i), 0)))
        out_shape.append(jax.ShapeDtypeStruct((n_rows, n_cols), BF16))
    return in_specs, out_specs, out_shape


def _mixer_step(cfg, i):
    return jnp.minimum(i, cfg.n_steps - 1)


def _state_in_spec(cfg, width, k):
    return pl.BlockSpec((None, cfg.state_rows, width),
                        lambda i: (k, _mixer_step(cfg, i) // cfg.steps_per_seq, 0))


def _state_spec(cfg, width):
    return pl.BlockSpec((cfg.state_rows, width), lambda i: (_mixer_step(cfg, i) // cfg.steps_per_seq, 0))


def _row_in_spec(cfg, width):
    return pl.BlockSpec((STEP_ROWS, width), lambda i: (_mixer_step(cfg, i), 0))


def _row_out_spec(cfg, width):
    if cfg.skewed:
        return pl.BlockSpec((STEP_ROWS, width), lambda i: (jnp.maximum(i - 1, 0), 0))
    return pl.BlockSpec((STEP_ROWS, width), lambda i: (i, 0))


def _grid(cfg):
    return (cfg.n_steps + 1,) if cfg.skewed else (cfg.n_steps,)


def _skew_scratch(cfg):
    return [pltpu.VMEM((STEP_ROWS, D_MODEL), F32)] if cfg.skewed else []


def _compiler_params():
    return pltpu.CompilerParams(dimension_semantics=("arbitrary",), vmem_limit_bytes=VMEM_LIMIT_BYTES)


def _even_layer(cfg, x, state, w, big, layer, cast_src=()):
    rows = x.shape[0]
    e = layer // 2
    consts = [(w["vec"], layer), (w["evec"], e), (w["wsb"], e), (w["wgate"], e)] + [(big[n], None) for n in BIG_NAMES]
    cast_in, cast_out, cast_shape = _cast_specs(cfg, cast_src)
    in_specs = ([_row_in_spec(cfg, D_MODEL)] + [_resident(c, k) for c, k in consts]
                + [_state_in_spec(cfg, 2 * D_B, e)] + cast_in)
    out_shape = [jax.ShapeDtypeStruct((rows, D_MODEL), F32),
                 jax.ShapeDtypeStruct((state.shape[1], 2 * D_B), F32)]
    out_specs = [_row_out_spec(cfg, D_MODEL), _state_spec(cfg, 2 * D_B)]
    if cfg.emit_v:
        out_shape.append(jax.ShapeDtypeStruct((rows, D_A), F32))
        out_specs.append(_row_in_spec(cfg, D_A))
    scratch = ([pltpu.VMEM((cfg.state_rows, D_B), F32), pltpu.VMEM((cfg.state_rows, D_B), F32)]
               + _skew_scratch(cfg))
    return pl.pallas_call(
        functools.partial(_even_layer_kernel, cfg, len(cast_src)),
        grid=_grid(cfg),
        in_specs=in_specs,
        out_specs=out_specs + cast_out,
        out_shape=out_shape + cast_shape,
        scratch_shapes=scratch,
        compiler_params=_compiler_params(),
        name="even_layer",
    )(x, *[c for c, _ in consts], state, *[arr for arr, _ in cast_src])


def _odd_layer(cfg, x, hist, w, big, layer, cast_src=()):
    rows = x.shape[0]
    o = layer // 2
    consts = [(w["vec"], layer)] + [(big[n], None) for n in BIG_NAMES]
    cast_in, cast_out, cast_shape = _cast_specs(cfg, cast_src)
    in_specs = ([_row_in_spec(cfg, D_MODEL)] + [_resident(c, k) for c, k in consts]
                + [_state_in_spec(cfg, D_C, o)] + cast_in)
    out_shape = [jax.ShapeDtypeStruct((rows, D_MODEL), F32),
                 jax.ShapeDtypeStruct((hist.shape[1], D_C), F32)]
    out_specs = [_row_out_spec(cfg, D_MODEL), _state_spec(cfg, D_C)]
    scratch = [pltpu.VMEM((cfg.state_rows, D_C), F32)] + _skew_scratch(cfg)
    return pl.pallas_call(
        functools.partial(_odd_layer_kernel, cfg, len(cast_src)),
        grid=_grid(cfg),
        in_specs=in_specs,
        out_specs=out_specs + cast_out,
        out_shape=out_shape + cast_shape,
        scratch_shapes=scratch,
        compiler_params=_compiler_params(),
        name="odd_layer",
    )(x, *[c for c, _ in consts], hist, *[arr for arr, _ in cast_src])


def _block_diag_gates(wa, wx):
    heads_per_half = B_HEADS // 2
    eye = jnp.eye(heads_per_half, dtype=wa.dtype)

    def bd(w):
        return jnp.einsum("eqhij,hg->eqhigj", w, eye).reshape(w.shape[0], 2, D_B // 2, D_B // 2)

    split = lambda w: w.reshape(w.shape[0], 2, heads_per_half, B_HEAD_DIM, B_HEAD_DIM)
    return jnp.concatenate([bd(split(wa)), bd(split(wx))], axis=-1).astype(BF16)


def _pad_state(rows):
    layers, n, r, c = rows.shape
    return jnp.pad(rows, ((0, 0), (0, 0), (SUBLANES - r, 0), (0, 0))).reshape(layers, n * SUBLANES, c)


def _small_weights(p):
    depth = p["norm_mix_pre"].shape[0]
    gains = jnp.stack([p["norm_mix_pre"], p["norm_mix_post"], p["norm_ffn_pre"], p["norm_ffn_post"]], axis=1)
    cconv = jnp.zeros((depth, C_CONV, D_C), F32).at[1::2].set(p["c_conv_w"])
    row = lambda v: v[:, None, :]
    evec = jnp.concatenate([row(p["a_v_gain"]), row(p["b_conv_b"]), row(p["b_ba"]), row(p["b_bx"]),
                            row(p["b_lambda"]), p["b_conv_w"]], axis=1)
    a_b_s = p["a_b_s"]
    bias = jnp.broadcast_to(a_b_s[..., None], a_b_s.shape + (A_HEAD_DIM,))
    return dict(vec=jnp.concatenate([gains, cconv], axis=1), evec=evec, wsb=jnp.stack([p["a_w_s"], bias], axis=1),
                wgate=_block_diag_gates(p["b_wa"], p["b_wx"]))


def _big_f32(p, layer):
    k = layer // 2
    if layer % 2 == 0:
        return [(p["w_in_even"], k), (p["w_out_even"], k), (p["mlp_up"], layer), (p["mlp_down"], layer)]
    return [(p["c_w_in"], k), (p["c_w_out"], k), (p["mlp_up"], layer), (p["mlp_down"], layer)]


def _trunk(x, b_state, c_conv, small, big, cfg, p=None):
    tails_b, tails_h, tails_c, v_rows = [], [], [], []
    for l in range(DEPTH):
        cast_src = _big_f32(p, l + 1) if p is not None and l + 1 < DEPTH else ()
        if l % 2 == 0:
            outs = _even_layer(cfg, x, b_state, small, big[l], l, cast_src)
            n_fixed = 3 if cfg.emit_v else 2
            tails_b.append(outs[1][:, :D_B])
            tails_h.append(outs[1][:, D_B:])
            if cfg.emit_v:
                v_rows.append(outs[2])
        else:
            outs = _odd_layer(cfg, x, c_conv, small, big[l], l, cast_src)
            n_fixed = 2
            tails_c.append(outs[1])
        x = outs[0]
        if cast_src:
            big[l + 1] = dict(zip(BIG_NAMES, outs[n_fixed:]))
    return x, tails_b, tails_h, tails_c, v_rows


def kernel(x_prompt, x_sample, cache_b_conv, state_b_h, cache_c_conv, norm_mix_pre, norm_mix_post, norm_ffn_pre, norm_ffn_post, w_in_even, a_v_gain, a_w_s, a_b_s, b_conv_w, b_conv_b, b_wa, b_ba, b_wx, b_bx, b_lambda, w_out_even, c_w_in, c_conv_w, c_w_out, mlp_up, mlp_down):
    batch, seq, _ = x_prompt.shape
    dec_batch, dec_seq, _ = x_sample.shape
    n_even, n_odd = w_in_even.shape[0], c_w_in.shape[0]
    p = dict(norm_mix_pre=norm_mix_pre, norm_mix_post=norm_mix_post, norm_ffn_pre=norm_ffn_pre,
             norm_ffn_post=norm_ffn_post, w_in_even=w_in_even, a_v_gain=a_v_gain, a_w_s=a_w_s, a_b_s=a_b_s,
             b_conv_w=b_conv_w, b_conv_b=b_conv_b, b_wa=b_wa, b_wx=b_wx, b_ba=b_ba, b_bx=b_bx,
             b_lambda=b_lambda, w_out_even=w_out_even, c_w_in=c_w_in, c_conv_w=c_conv_w, c_w_out=c_w_out,
             mlp_up=mlp_up, mlp_down=mlp_down)
    small = _small_weights(p)
    big = {0: {name: arr[k].astype(BF16) for name, (arr, k) in zip(BIG_NAMES, _big_f32(p, 0))}}

    cfg_p = TileCfg(n_seq=1, seq_rows=ROW_TILE, steps_per_seq=seq // STEP_ROWS, gate_chunk=A_CHUNK,
                    emit_v=False, n_steps=batch * seq // STEP_ROWS, skewed=True)
    zb = jnp.zeros((n_even, batch * SUBLANES, 2 * D_B), F32)
    zc = jnp.zeros((n_odd, batch * SUBLANES, D_C), F32)
    y_p, tb_p, th_p, tc_p, _ = _trunk(x_prompt.reshape(batch * seq, D_MODEL), zb, zc, small, big, cfg_p, p)

    cfg_s = TileCfg(n_seq=ROW_TILE // dec_seq, seq_rows=dec_seq, steps_per_seq=1, gate_chunk=dec_seq,
                    emit_v=True, n_steps=dec_batch * dec_seq // STEP_ROWS, skewed=False)
    sb = jnp.concatenate([_pad_state(cache_b_conv), _pad_state(state_b_h[:, :, None, :])], axis=-1)
    sc = _pad_state(cache_c_conv)
    y_s, tb_s, th_s, tc_s, v_s = _trunk(x_sample.reshape(dec_batch * dec_seq, D_MODEL), sb, sc, small, big, cfg_s)

    def tails(ts, n, keep):
        t = jnp.stack(ts).reshape(len(ts), n, SUBLANES, -1)
        return t[:, :, SUBLANES - keep:, :]

    return (y_p.reshape(batch, seq, D_MODEL),
            y_s.reshape(dec_batch, dec_seq, D_MODEL),
            jnp.stack(v_s).reshape(n_even, dec_batch, dec_seq, D_A),
            tails(tb_p, batch, B_CONV - 1),
            tails(th_p, batch, 1)[:, :, 0, :],
            tails(tc_p, batch, C_CONV - 1),
            tails(tb_s, dec_batch, B_CONV - 1),
            tails(th_s, dec_batch, 1)[:, :, 0, :],
            tails(tc_s, dec_batch, C_CONV - 1))
```

```python
import functools
from typing import NamedTuple

import jax
import jax.numpy as jnp
from jax import lax
from jax.experimental import pallas as pl
from jax.experimental.pallas import tpu as pltpu

D_MODEL = 1024
DEPTH = 4
CHUNK = 64
A_CHUNK = 128
D_A = D_MODEL // 2
A_HEADS = 4
A_HEAD_DIM = D_A // A_HEADS
D_B = D_MODEL // 2
B_HEADS = 8
B_HEAD_DIM = D_B // B_HEADS
B_CONV = 4
LRU_C = 8.0
D_C = D_MODEL
C_CONV = 3
D_FF = 4 * D_MODEL
EPS = 1e-6

SUBLANES = 8
V7X_VMEM_BYTES = 64 * 1024 * 1024
VMEM_LIMIT_BYTES = V7X_VMEM_BYTES - 8 * 1024 * 1024
ROW_TILE = 512
TILES_PER_STEP = 1
STEP_ROWS = ROW_TILE * TILES_PER_STEP

F32 = jnp.float32
BF16 = jnp.bfloat16


class TileCfg(NamedTuple):
    n_seq: int
    seq_rows: int
    steps_per_seq: int
    gate_chunk: int
    emit_v: bool
    n_steps: int
    skewed: bool

    @property
    def state_rows(self):
        groups = TILES_PER_STEP if self.n_seq > 1 else 1
        return groups * self.n_seq * SUBLANES

    def tile_state(self, t):
        if self.n_seq == 1:
            return slice(0, SUBLANES)
        return slice(t * self.n_seq * SUBLANES, (t + 1) * self.n_seq * SUBLANES)


def _rmsnorm(x, g):
    ms = jnp.mean(x * x, axis=-1, keepdims=True)
    return x * lax.rsqrt(ms + EPS) * g


def _shift_rows(x, hist, k, cfg):
    if k == 0:
        return x
    rolled = pltpu.roll(x, k, 0)
    row = lax.broadcasted_iota(jnp.int32, (SUBLANES, x.shape[1]), 0)
    pieces = []
    for s in range(cfg.n_seq):
        lo = s * cfg.seq_rows
        h = pltpu.roll(hist[s * SUBLANES:(s + 1) * SUBLANES], k, 0)
        pieces.append(jnp.where(row < k, h, rolled[lo:lo + SUBLANES]))
        pieces.append(rolled[lo + SUBLANES:lo + cfg.seq_rows])
    return jnp.concatenate(pieces, axis=0)


def _causal_conv(x, hist, w, cfg):
    width = w.shape[0]
    y = w[0:1] * _shift_rows(x, hist, width - 1, cfg)
    for k in range(1, width):
        y = y + w[k:k + 1] * _shift_rows(x, hist, width - 1 - k, cfg)
    return y


def _seq_tails(x, cfg):
    pieces = [x[(s + 1) * cfg.seq_rows - SUBLANES:(s + 1) * cfg.seq_rows] for s in range(cfg.n_seq)]
    return pieces[0] if len(pieces) == 1 else jnp.concatenate(pieces, axis=0)


def _linear_scan(a, b, h_prev, cfg):
    row = lax.broadcasted_iota(jnp.int32, a.shape, 0) & (SUBLANES - 1)
    d = 1
    while d < SUBLANES:
        keep = row >= d
        a_sh = pltpu.roll(a, d, 0)
        b_sh = pltpu.roll(b, d, 0)
        b = jnp.where(keep, a * b_sh + b, b)
        a = jnp.where(keep, a * a_sh, a)
        d *= 2
    groups_per_seq = cfg.seq_rows // SUBLANES
    out = []
    for s in range(cfg.n_seq):
        carry = h_prev[s * SUBLANES + SUBLANES - 1:(s + 1) * SUBLANES]
        for g in range(groups_per_seq):
            lo = s * cfg.seq_rows + g * SUBLANES
            hg = b[lo:lo + SUBLANES] + a[lo:lo + SUBLANES] * carry
            out.append(hg)
            carry = hg[SUBLANES - 1:SUBLANES]
    return jnp.concatenate(out, axis=0)


def _spatial_gating(v, ws_ref, bias_ref, cfg):
    lc = cfg.gate_chunk
    n_chunks = v.shape[0] // lc
    vb = v.astype(BF16)
    pi = lax.shift_right_logical(lax.broadcasted_iota(jnp.int32, (lc, lc), 0), 6)
    pj = lax.shift_right_logical(lax.broadcasted_iota(jnp.int32, (lc, lc), 1), 6)
    mask = pj <= pi
    heads = []
    for h in range(A_HEADS):
        lanes = slice(h * A_HEAD_DIM, (h + 1) * A_HEAD_DIM)
        w = jnp.where(mask, ws_ref[h, 0:lc, 0:lc], 0.0).astype(BF16)
        vh = jnp.concatenate([vb[c * lc:(c + 1) * lc, lanes] for c in range(n_chunks)], axis=1)
        sh = jnp.dot(w, vh, preferred_element_type=F32)
        bias = bias_ref[h, 0:lc, :]
        heads.append(jnp.concatenate(
            [sh[:, c * A_HEAD_DIM:(c + 1) * A_HEAD_DIM] + bias for c in range(n_chunks)], axis=0))
    return jnp.concatenate(heads, axis=1)


def _mixer_in(x, gpre_ref, win_ref):
    hn = _rmsnorm(x, gpre_ref[...]).astype(BF16)
    return jnp.dot(hn, win_ref[...], preferred_element_type=F32)


def _mixer_out(x, mix, wout_ref, gpost_ref):
    m = jnp.dot(mix, wout_ref[...], preferred_element_type=F32)
    return x + _rmsnorm(m, gpost_ref[...])


def _mlp_up(x, gpre_ref, up_ref):
    hn = _rmsnorm(x, gpre_ref[...]).astype(BF16)
    return jnp.dot(hn, up_ref[...], preferred_element_type=F32)


def _mlp_down(x, hid, down_ref, gpost_ref):
    act = jnp.square(jnp.maximum(hid, 0.0)).astype(BF16)
    f = jnp.dot(act, down_ref[...], preferred_element_type=F32)
    return x + _rmsnorm(f, gpost_ref[...])


def _keep_state(cfg, new, old):
    return jnp.where(pl.program_id(0) < cfg.n_steps, new, old) if cfg.skewed else new


def _even_mixer(cfg, x, z, refs, states, tile, between):
    (vgain_ref, ws_ref, bias_ref, convw_ref, convb_ref, wgate_ref, ba_ref, bx_ref, lam_ref, wout_ref,
     gpost_ref) = refs
    hist_scr, h_scr, tailx_ref, tailh_ref, v_ref = states
    srows = cfg.tile_state(tile)
    u = jax.nn.gelu(z[:, 0:D_A])
    v = _rmsnorm(jax.nn.gelu(z[:, D_A:2 * D_A]), vgain_ref[...])
    xb = z[:, 2 * D_A:2 * D_A + D_B]
    gb = z[:, 2 * D_A + D_B:]
    if v_ref is not None:
        v_ref[tile * ROW_TILE:(tile + 1) * ROW_TILE, :] = v
    hist = hist_scr[srows, :]
    xc = _causal_conv(xb, hist, convw_ref[...], cfg) + convb_ref[...]
    new_hist = _keep_state(cfg, _seq_tails(xb, cfg), hist)
    hist_scr[srows, :] = new_hist
    tailx_ref[srows, :] = new_hist

    gate = _spatial_gating(v, ws_ref, bias_ref, cfg)
    half = D_B // 2
    xcb = xc.astype(BF16)
    g0 = jnp.dot(xcb[:, :half], wgate_ref[0], preferred_element_type=F32)
    g1 = jnp.dot(xcb[:, half:], wgate_ref[1], preferred_element_type=F32)
    between()

    a_out = u * gate
    r = jax.nn.sigmoid(jnp.concatenate([g0[:, :half], g1[:, :half]], axis=1) + ba_ref[...])
    ig = jax.nn.sigmoid(jnp.concatenate([g0[:, half:], g1[:, half:]], axis=1) + bx_ref[...])
    nl = -lam_ref[...]
    softplus = jnp.maximum(nl, 0.0) + jnp.log1p(jnp.exp(-jnp.abs(nl)))
    log_a = -LRU_C * r * softplus
    a = jnp.exp(log_a)
    bterm = jnp.sqrt(-jnp.tanh(log_a) * (a * a + 1.0)) * ig * xc
    h_prev = h_scr[srows, :]
    hseq = _linear_scan(a, bterm, h_prev, cfg)
    new_h = _keep_state(cfg, _seq_tails(hseq, cfg), h_prev)
    h_scr[srows, :] = new_h
    tailh_ref[srows, :] = new_h

    b_out = hseq * jax.nn.gelu(gb)
    mix = jnp.concatenate([a_out, b_out], axis=1).astype(BF16)
    return _mixer_out(x, mix, wout_ref, gpost_ref)


def _odd_mixer(cfg, x, z, refs, states, tile, between):
    convw_ref, wout_ref, gpost_ref = refs
    hist_scr, tail_ref = states
    srows = cfg.tile_state(tile)
    between()
    bg = z[:, 0:D_C]
    p = z[:, D_C:2 * D_C] * z[:, 2 * D_C:]
    hist = hist_scr[srows, :]
    y = _causal_conv(p, hist, convw_ref[...], cfg)
    new_hist = _keep_state(cfg, _seq_tails(p, cfg), hist)
    hist_scr[srows, :] = new_hist
    tail_ref[srows, :] = new_hist
    return _mixer_out(x, (bg * y).astype(BF16), wout_ref, gpost_ref)


def _paired(mixer, x, z_of, mlp_refs, mlp_x, write_mlp):
    gfpre_ref, up_ref, down_ref, gfpost_ref = mlp_refs
    if x is not None and mlp_x is not None:
        hn = _rmsnorm(mlp_x, gfpre_ref[...]).astype(BF16)
        hid0 = jnp.dot(hn, up_ref[:, 0:D_FF // 2], preferred_element_type=F32)
        z = z_of(x)
        hid1 = jnp.dot(hn, up_ref[:, D_FF // 2:], preferred_element_type=F32)
        hid = jnp.concatenate([hid0, hid1], axis=1)
    else:
        z = z_of(x) if x is not None else None
        hid = _mlp_up(mlp_x, gfpre_ref, up_ref) if mlp_x is not None else None

    def mlp_down():
        if mlp_x is not None:
            write_mlp(_mlp_down(mlp_x, hid, down_ref, gfpost_ref))

    if x is None:
        mlp_down()
        return None
    return mixer(x, z, mlp_down)


def _layer_body(cfg, mixer, z_of, mlp_refs, x_ref, xo_ref, x1_scr):
    rows = lambda t: slice(t * ROW_TILE, (t + 1) * ROW_TILE)

    def write_out(t):
        def write(val):
            xo_ref[rows(t), :] = val
        return write

    if not cfg.skewed:
        x1_prev = None
        for t in range(TILES_PER_STEP + 1):
            x = x_ref[rows(t), :] if t < TILES_PER_STEP else None
            x1 = _paired(functools.partial(mixer, t), x, z_of, mlp_refs, x1_prev,
                         write_out(t - 1) if t > 0 else None)
            x1_prev = x1
        return

    @pl.when(pl.program_id(0) == 0)
    def _():
        x1_scr[...] = jnp.zeros_like(x1_scr)

    for t in range(TILES_PER_STEP):
        x1_scr[rows(t), :] = _paired(functools.partial(mixer, t), x_ref[rows(t), :], z_of, mlp_refs,
                                     x1_scr[rows(t), :], write_out(t))


def _load_states_at_sequence_start(cfg, pairs):
    i = pl.program_id(0)

    @pl.when(jnp.logical_and(lax.rem(i, cfg.steps_per_seq) == 0, i < cfg.n_steps))
    def _():
        for scr, src in pairs:
            scr[...] = src[...]


def _cast_next_layer_weights(cast_in_refs, cast_out_refs):
    for src, dst in zip(cast_in_refs, cast_out_refs):
        dst[...] = src[...].astype(BF16)


BIG_NAMES = ("win", "wout", "up", "down")

ROW_GPRE, ROW_GPOST, ROW_GFPRE, ROW_GFPOST, ROW_CCONV = 0, 1, 2, 3, 4
N_VEC_ROWS = ROW_CCONV + C_CONV
ROW_VGAIN, ROW_CONVB, ROW_BA, ROW_BX, ROW_LAM, ROW_BCONV = 0, 1, 2, 3, 4, 5
N_EVEN_VEC_ROWS = ROW_BCONV + B_CONV


def _row(ref, r, n=1):
    return ref.at[r:r + n, :]


def _even_layer_kernel(cfg, n_cast, *refs):
    refs = list(refs)
    x_ref, vec_ref, evec_ref, wsb_ref, wgate_ref, win_ref, wout_ref, up_ref, down_ref, state_in_ref = refs[:10]
    cast_in_refs = refs[10:10 + n_cast]
    del refs[:10 + n_cast]
    xo_ref, tails_ref = refs[:2]
    del refs[:2]
    v_ref = refs.pop(0) if cfg.emit_v else None
    gpre_ref, gpost_ref = _row(vec_ref, ROW_GPRE), _row(vec_ref, ROW_GPOST)
    gfpre_ref, gfpost_ref = _row(vec_ref, ROW_GFPRE), _row(vec_ref, ROW_GFPOST)
    vgain_ref, convb_ref = _row(evec_ref, ROW_VGAIN), _row(evec_ref, ROW_CONVB)
    ba_ref, bx_ref, lam_ref = _row(evec_ref, ROW_BA), _row(evec_ref, ROW_BX), _row(evec_ref, ROW_LAM)
    convw_ref = _row(evec_ref, ROW_BCONV, B_CONV)
    ws_ref, bias_ref = wsb_ref.at[0], wsb_ref.at[1]
    hist_in_ref, h_in_ref = state_in_ref.at[:, 0:D_B], state_in_ref.at[:, D_B:2 * D_B]
    tailx_ref, tailh_ref = tails_ref.at[:, 0:D_B], tails_ref.at[:, D_B:2 * D_B]
    cast_out_refs = refs[:n_cast]
    del refs[:n_cast]
    hist_scr, h_scr = refs[:2]
    x1_scr = refs[2] if cfg.skewed else None

    mixer_refs = (vgain_ref, ws_ref, bias_ref, convw_ref, convb_ref, wgate_ref, ba_ref, bx_ref, lam_ref,
                  wout_ref, gpost_ref)
    states = (hist_scr, h_scr, tailx_ref, tailh_ref, v_ref)
    _load_states_at_sequence_start(cfg, [(hist_scr, hist_in_ref), (h_scr, h_in_ref)])

    def mixer(tile, x, z, between):
        return _even_mixer(cfg, x, z, mixer_refs, states, tile, between)

    _layer_body(cfg, mixer, lambda x: _mixer_in(x, gpre_ref, win_ref), (gfpre_ref, up_ref, down_ref, gfpost_ref),
                x_ref, xo_ref, x1_scr)
    _cast_next_layer_weights(cast_in_refs, cast_out_refs)


def _odd_layer_kernel(cfg, n_cast, *refs):
    refs = list(refs)
    x_ref, vec_ref, win_ref, wout_ref, up_ref, down_ref, hist_in_ref = refs[:7]
    cast_in_refs = refs[7:7 + n_cast]
    del refs[:7 + n_cast]
    gpre_ref, gpost_ref = _row(vec_ref, ROW_GPRE), _row(vec_ref, ROW_GPOST)
    gfpre_ref, gfpost_ref = _row(vec_ref, ROW_GFPRE), _row(vec_ref, ROW_GFPOST)
    convw_ref = _row(vec_ref, ROW_CCONV, C_CONV)
    xo_ref, tail_ref = refs[:2]
    del refs[:2]
    cast_out_refs = refs[:n_cast]
    del refs[:n_cast]
    hist_scr = refs[0]
    x1_scr = refs[1] if cfg.skewed else None

    _load_states_at_sequence_start(cfg, [(hist_scr, hist_in_ref)])

    def mixer(tile, x, z, between):
        return _odd_mixer(cfg, x, z, (convw_ref, wout_ref, gpost_ref), (hist_scr, tail_ref), tile, between)

    _layer_body(cfg, mixer, lambda x: _mixer_in(x, gpre_ref, win_ref), (gfpre_ref, up_ref, down_ref, gfpost_ref),
                x_ref, xo_ref, x1_scr)
    _cast_next_layer_weights(cast_in_refs, cast_out_refs)


def _resident(arr, layer):
    if layer is None:
        zeros = (0,) * arr.ndim
        return pl.BlockSpec(arr.shape, lambda i: zeros, pipeline_mode=pl.Buffered(1))
    zeros = (0,) * (arr.ndim - 1)
    return pl.BlockSpec((None,) + arr.shape[1:], lambda i: (layer,) + zeros, pipeline_mode=pl.Buffered(1))


def _cast_specs(cfg, cast_src):
    in_specs, out_specs, out_shape = [], [], []
    for arr, k in cast_src:
        _, n_rows, n_cols = arr.shape
        chunk = n_rows // cfg.n_steps
        in_specs.append(pl.BlockSpec((None, chunk, n_cols), lambda i, _k=k: (_k, _mixer_step(cfg, i), 0)))
        out_specs.append(pl.BlockSpec((chunk, n_cols), lambda i: (_mixer_step(cfg, i), 0)))
        out_shape.append(jax.ShapeDtypeStruct((n_rows, n_cols), BF16))
    return in_specs, out_specs, out_shape


def _mixer_step(cfg, i):
    return jnp.minimum(i, cfg.n_steps - 1)


def _state_in_spec(cfg, width, k):
    return pl.BlockSpec((None, cfg.state_rows, width),
                        lambda i: (k, _mixer_step(cfg, i) // cfg.steps_per_seq, 0))


def _state_spec(cfg, width):
    return pl.BlockSpec((cfg.state_rows, width), lambda i: (_mixer_step(cfg, i) // cfg.steps_per_seq, 0))


def _row_in_spec(cfg, width):
    return pl.BlockSpec((STEP_ROWS, width), lambda i: (_mixer_step(cfg, i), 0))


def _row_out_spec(cfg, width):
    if cfg.skewed:
        return pl.BlockSpec((STEP_ROWS, width), lambda i: (jnp.maximum(i - 1, 0), 0))
    return pl.BlockSpec((STEP_ROWS, width), lambda i: (i, 0))


def _grid(cfg):
    return (cfg.n_steps + 1,) if cfg.skewed else (cfg.n_steps,)


def _skew_scratch(cfg):
    return [pltpu.VMEM((STEP_ROWS, D_MODEL), F32)] if cfg.skewed else []


def _compiler_params():
    return pltpu.CompilerParams(dimension_semantics=("arbitrary",), vmem_limit_bytes=VMEM_LIMIT_BYTES)


def _even_layer(cfg, x, state, w, big, layer, cast_src=()):
    rows = x.shape[0]
    e = layer // 2
    consts = [(w["vec"], layer), (w["evec"], e), (w["wsb"], e), (w["wgate"], e)] + [(big[n], None) for n in BIG_NAMES]
    cast_in, cast_out, cast_shape = _cast_specs(cfg, cast_src)
    in_specs = ([_row_in_spec(cfg, D_MODEL)] + [_resident(c, k) for c, k in consts]
                + [_state_in_spec(cfg, 2 * D_B, e)] + cast_in)
    out_shape = [jax.ShapeDtypeStruct((rows, D_MODEL), F32),
                 jax.ShapeDtypeStruct((state.shape[1], 2 * D_B), F32)]
    out_specs = [_row_out_spec(cfg, D_MODEL), _state_spec(cfg, 2 * D_B)]
    if cfg.emit_v:
        out_shape.append(jax.ShapeDtypeStruct((rows, D_A), F32))
        out_specs.append(_row_in_spec(cfg, D_A))
    scratch = ([pltpu.VMEM((cfg.state_rows, D_B), F32), pltpu.VMEM((cfg.state_rows, D_B), F32)]
               + _skew_scratch(cfg))
    return pl.pallas_call(
        functools.partial(_even_layer_kernel, cfg, len(cast_src)),
        grid=_grid(cfg),
        in_specs=in_specs,
        out_specs=out_specs + cast_out,
        out_shape=out_shape + cast_shape,
        scratch_shapes=scratch,
        compiler_params=_compiler_params(),
        name="even_layer",
    )(x, *[c for c, _ in consts], state, *[arr for arr, _ in cast_src])


def _odd_layer(cfg, x, hist, w, big, layer, cast_src=()):
    rows = x.shape[0]
    o = layer // 2
    consts = [(w["vec"], layer)] + [(big[n], None) for n in BIG_NAMES]
    cast_in, cast_out, cast_shape = _cast_specs(cfg, cast_src)
    in_specs = ([_row_in_spec(cfg, D_MODEL)] + [_resident(c, k) for c, k in consts]
                + [_state_in_spec(cfg, D_C, o)] + cast_in)
    out_shape = [jax.ShapeDtypeStruct((rows, D_MODEL), F32),
                 jax.ShapeDtypeStruct((hist.shape[1], D_C), F32)]
    out_specs = [_row_out_spec(cfg, D_MODEL), _state_spec(cfg, D_C)]
    scratch = [pltpu.VMEM((cfg.state_rows, D_C), F32)] + _skew_scratch(cfg)
    return pl.pallas_call(
        functools.partial(_odd_layer_kernel, cfg, len(cast_src)),
        grid=_grid(cfg),
        in_specs=in_specs,
        out_specs=out_specs + cast_out,
        out_shape=out_shape + cast_shape,
        scratch_shapes=scratch,
        compiler_params=_compiler_params(),
        name="odd_layer",
    )(x, *[c for c, _ in consts], hist, *[arr for arr, _ in cast_src])


def _block_diag_gates(wa, wx):
    heads_per_half = B_HEADS // 2
    eye = jnp.eye(heads_per_half, dtype=wa.dtype)

    def bd(w):
        return jnp.einsum("eqhij,hg->eqhigj", w, eye).reshape(w.shape[0], 2, D_B // 2, D_B // 2)

    split = lambda w: w.reshape(w.shape[0], 2, heads_per_half, B_HEAD_DIM, B_HEAD_DIM)
    return jnp.concatenate([bd(split(wa)), bd(split(wx))], axis=-1).astype(BF16)


def _pad_state(rows):
    layers, n, r, c = rows.shape
    return jnp.pad(rows, ((0, 0), (0, 0), (SUBLANES - r, 0), (0, 0))).reshape(layers, n * SUBLANES, c)


def _small_weights(p):
    depth = p["norm_mix_pre"].shape[0]
    gains = jnp.stack([p["norm_mix_pre"], p["norm_mix_post"], p["norm_ffn_pre"], p["norm_ffn_post"]], axis=1)
    cconv = jnp.zeros((depth, C_CONV, D_C), F32).at[1::2].set(p["c_conv_w"])
    row = lambda v: v[:, None, :]
    evec = jnp.concatenate([row(p["a_v_gain"]), row(p["b_conv_b"]), row(p["b_ba"]), row(p["b_bx"]),
                            row(p["b_lambda"]), p["b_conv_w"]], axis=1)
    a_b_s = p["a_b_s"]
    bias = jnp.broadcast_to(a_b_s[..., None], a_b_s.shape + (A_HEAD_DIM,))
    return dict(vec=jnp.concatenate([gains, cconv], axis=1), evec=evec, wsb=jnp.stack([p["a_w_s"], bias], axis=1),
                wgate=_block_diag_gates(p["b_wa"], p["b_wx"]))


def _big_f32(p, layer):
    k = layer // 2
    if layer % 2 == 0:
        return [(p["w_in_even"], k), (p["w_out_even"], k), (p["mlp_up"], layer), (p["mlp_down"], layer)]
    return [(p["c_w_in"], k), (p["c_w_out"], k), (p["mlp_up"], layer), (p["mlp_down"], layer)]


def _trunk(x, b_state, c_conv, small, big, cfg, p=None):
    tails_b, tails_h, tails_c, v_rows = [], [], [], []
    for l in range(DEPTH):
        cast_src = _big_f32(p, l + 1) if p is not None and l + 1 < DEPTH else ()
        if l % 2 == 0:
            outs = _even_layer(cfg, x, b_state, small, big[l], l, cast_src)
            n_fixed = 3 if cfg.emit_v else 2
            tails_b.append(outs[1][:, :D_B])
            tails_h.append(outs[1][:, D_B:])
            if cfg.emit_v:
                v_rows.append(outs[2])
        else:
            outs = _odd_layer(cfg, x, c_conv, small, big[l], l, cast_src)
            n_fixed = 2
            tails_c.append(outs[1])
        x = outs[0]
        if cast_src:
            big[l + 1] = dict(zip(BIG_NAMES, outs[n_fixed:]))
    return x, tails_b, tails_h, tails_c, v_rows


def kernel(x_prompt, x_sample, cache_b_conv, state_b_h, cache_c_conv, norm_mix_pre, norm_mix_post, norm_ffn_pre, norm_ffn_post, w_in_even, a_v_gain, a_w_s, a_b_s, b_conv_w, b_conv_b, b_wa, b_ba, b_wx, b_bx, b_lambda, w_out_even, c_w_in, c_conv_w, c_w_out, mlp_up, mlp_down):
    batch, seq, _ = x_prompt.shape
    dec_batch, dec_seq, _ = x_sample.shape
    n_even, n_odd = w_in_even.shape[0], c_w_in.shape[0]
    p = dict(norm_mix_pre=norm_mix_pre, norm_mix_post=norm_mix_post, norm_ffn_pre=norm_ffn_pre,
             norm_ffn_post=norm_ffn_post, w_in_even=w_in_even, a_v_gain=a_v_gain, a_w_s=a_w_s, a_b_s=a_b_s,
             b_conv_w=b_conv_w, b_conv_b=b_conv_b, b_wa=b_wa, b_wx=b_wx, b_ba=b_ba, b_bx=b_bx,
             b_lambda=b_lambda, w_out_even=w_out_even, c_w_in=c_w_in, c_conv_w=c_conv_w, c_w_out=c_w_out,
             mlp_up=mlp_up, mlp_down=mlp_down)
    small = _small_weights(p)
    big = {0: {name: arr[k].astype(BF16) for name, (arr, k) in zip(BIG_NAMES, _big_f32(p, 0))}}

    cfg_p = TileCfg(n_seq=1, seq_rows=ROW_TILE, steps_per_seq=seq // STEP_ROWS, gate_chunk=A_CHUNK,
                    emit_v=False, n_steps=batch * seq // STEP_ROWS, skewed=True)
    zb = jnp.zeros((n_even, batch * SUBLANES, 2 * D_B), F32)
    zc = jnp.zeros((n_odd, batch * SUBLANES, D_C), F32)
    y_p, tb_p, th_p, tc_p, _ = _trunk(x_prompt.reshape(batch * seq, D_MODEL), zb, zc, small, big, cfg_p, p)

    cfg_s = TileCfg(n_seq=ROW_TILE // dec_seq, seq_rows=dec_seq, steps_per_seq=1, gate_chunk=dec_seq,
                    emit_v=True, n_steps=dec_batch * dec_seq // STEP_ROWS, skewed=False)
    sb = jnp.concatenate([_pad_state(cache_b_conv), _pad_state(state_b_h[:, :, None, :])], axis=-1)
    sc = _pad_state(cache_c_conv)
    y_s, tb_s, th_s, tc_s, v_s = _trunk(x_sample.reshape(dec_batch * dec_seq, D_MODEL), sb, sc, small, big, cfg_s)

    def tails(ts, n, keep):
        t = jnp.stack(ts).reshape(len(ts), n, SUBLANES, -1)
        return t[:, :, SUBLANES - keep:, :]

    return (y_p.reshape(batch, seq, D_MODEL),
            y_s.reshape(dec_batch, dec_seq, D_MODEL),
            jnp.stack(v_s).reshape(n_even, dec_batch, dec_seq, D_A),
            tails(tb_p, batch, B_CONV - 1),
            tails(th_p, batch, 1)[:, :, 0, :],
            tails(tc_p, batch, C_CONV - 1),
            tails(tb_s, dec_batch, B_CONV - 1),
            tails(th_s, dec_batch, 1)[:, :, 0, :],
            tails(tc_s, dec_batch, C_CONV - 1))
```

```python
import functools
from typing import NamedTuple

import jax
import jax.numpy as jnp
from jax import lax
from jax.experimental import pallas as pl
from jax.experimental.pallas import tpu as pltpu

D_MODEL = 1024
DEPTH = 4
CHUNK = 64
A_CHUNK = 128
D_A = D_MODEL // 2
A_HEADS = 4
A_HEAD_DIM = D_A // A_HEADS
D_B = D_MODEL // 2
B_HEADS = 8
B_HEAD_DIM = D_B // B_HEADS
B_CONV = 4
LRU_C = 8.0
D_C = D_MODEL
C_CONV = 3
D_FF = 4 * D_MODEL
EPS = 1e-6

SUBLANES = 8
V7X_VMEM_BYTES = 64 * 1024 * 1024
VMEM_LIMIT_BYTES = V7X_VMEM_BYTES - 8 * 1024 * 1024
ROW_TILE = 512
TILES_PER_STEP = 1
STEP_ROWS = ROW_TILE * TILES_PER_STEP

F32 = jnp.float32
BF16 = jnp.bfloat16


class TileCfg(NamedTuple):
    n_seq: int
    seq_rows: int
    steps_per_seq: int
    gate_chunk: int
    emit_v: bool
    n_steps: int
    skewed: bool

    @property
    def state_rows(self):
        groups = TILES_PER_STEP if self.n_seq > 1 else 1
        return groups * self.n_seq * SUBLANES

    def tile_state(self, t):
        if self.n_seq == 1:
            return slice(0, SUBLANES)
        return slice(t * self.n_seq * SUBLANES, (t + 1) * self.n_seq * SUBLANES)


def _rmsnorm(x, g):
    ms = jnp.mean(x * x, axis=-1, keepdims=True)
    return x * lax.rsqrt(ms + EPS) * g


def _shift_rows(x, hist, k, cfg):
    if k == 0:
        return x
    rolled = pltpu.roll(x, k, 0)
    row = lax.broadcasted_iota(jnp.int32, (SUBLANES, x.shape[1]), 0)
    pieces = []
    for s in range(cfg.n_seq):
        lo = s * cfg.seq_rows
        h = pltpu.roll(hist[s * SUBLANES:(s + 1) * SUBLANES], k, 0)
        pieces.append(jnp.where(row < k, h, rolled[lo:lo + SUBLANES]))
        pieces.append(rolled[lo + SUBLANES:lo + cfg.seq_rows])
    return jnp.concatenate(pieces, axis=0)


def _causal_conv(x, hist, w, cfg):
    width = w.shape[0]
    y = w[0:1] * _shift_rows(x, hist, width - 1, cfg)
    for k in range(1, width):
        y = y + w[k:k + 1] * _shift_rows(x, hist, width - 1 - k, cfg)
    return y


def _seq_tails(x, cfg):
    pieces = [x[(s + 1) * cfg.seq_rows - SUBLANES:(s + 1) * cfg.seq_rows] for s in range(cfg.n_seq)]
    return pieces[0] if len(pieces) == 1 else jnp.concatenate(pieces, axis=0)


def _linear_scan(a, b, h_prev, cfg):
    row = lax.broadcasted_iota(jnp.int32, a.shape, 0) & (SUBLANES - 1)
    d = 1
    while d < SUBLANES:
        keep = row >= d
        a_sh = pltpu.roll(a, d, 0)
        b_sh = pltpu.roll(b, d, 0)
        b = jnp.where(keep, a * b_sh + b, b)
        a = jnp.where(keep, a * a_sh, a)
        d *= 2
    groups_per_seq = cfg.seq_rows // SUBLANES
    out = []
    for s in range(cfg.n_seq):
        carry = h_prev[s * SUBLANES + SUBLANES - 1:(s + 1) * SUBLANES]
        for g in range(groups_per_seq):
            lo = s * cfg.seq_rows + g * SUBLANES
            hg = b[lo:lo + SUBLANES] + a[lo:lo + SUBLANES] * carry
            out.append(hg)
            carry = hg[SUBLANES - 1:SUBLANES]
    return jnp.concatenate(out, axis=0)


def _spatial_gating(v, ws_ref, bias_ref, cfg):
    lc = cfg.gate_chunk
    n_chunks = v.shape[0] // lc
    vb = v.astype(BF16)
    pi = lax.shift_right_logical(lax.broadcasted_iota(jnp.int32, (lc, lc), 0), 6)
    pj = lax.shift_right_logical(lax.broadcasted_iota(jnp.int32, (lc, lc), 1), 6)
    mask = pj <= pi
    heads = []
    for h in range(A_HEADS):
        lanes = slice(h * A_HEAD_DIM, (h + 1) * A_HEAD_DIM)
        w = jnp.where(mask, ws_ref[h, 0:lc, 0:lc], 0.0).astype(BF16)
        vh = jnp.concatenate([vb[c * lc:(c + 1) * lc, lanes] for c in range(n_chunks)], axis=1)
        sh = jnp.dot(w, vh, preferred_element_type=F32)
        bias = bias_ref[h, 0:lc, :]
        heads.append(jnp.concatenate(
            [sh[:, c * A_HEAD_DIM:(c + 1) * A_HEAD_DIM] + bias for c in range(n_chunks)], axis=0))
    return jnp.concatenate(heads, axis=1)


def _mixer_in(x, gpre_ref, win_ref):
    hn = _rmsnorm(x, gpre_ref[...]).astype(BF16)
    return jnp.dot(hn, win_ref[...], preferred_element_type=F32)


def _mixer_out(x, mix, wout_ref, gpost_ref):
    m = jnp.dot(mix, wout_ref[...], preferred_element_type=F32)
    return x + _rmsnorm(m, gpost_ref[...])


def _mlp_up(x, gpre_ref, up_ref):
    hn = _rmsnorm(x, gpre_ref[...]).astype(BF16)
    return jnp.dot(hn, up_ref[...], preferred_element_type=F32)


def _mlp_down(x, hid, down_ref, gpost_ref):
    act = jnp.square(jnp.maximum(hid, 0.0)).astype(BF16)
    f = jnp.dot(act, down_ref[...], preferred_element_type=F32)
    return x + _rmsnorm(f, gpost_ref[...])


def _keep_state(cfg, new, old):
    return jnp.where(pl.program_id(0) < cfg.n_steps, new, old) if cfg.skewed else new


def _even_mixer(cfg, x, z, refs, states, tile, between):
    (vgain_ref, ws_ref, bias_ref, convw_ref, convb_ref, wgate_ref, ba_ref, bx_ref, lam_ref, wout_ref,
     gpost_ref) = refs
    hist_scr, h_scr, tailx_ref, tailh_ref, v_ref = states
    srows = cfg.tile_state(tile)
    u = jax.nn.gelu(z[:, 0:D_A])
    v = _rmsnorm(jax.nn.gelu(z[:, D_A:2 * D_A]), vgain_ref[...])
    xb = z[:, 2 * D_A:2 * D_A + D_B]
    gb = z[:, 2 * D_A + D_B:]
    if v_ref is not None:
        v_ref[tile * ROW_TILE:(tile + 1) * ROW_TILE, :] = v
    hist = hist_scr[srows, :]
    xc = _causal_conv(xb, hist, convw_ref[...], cfg) + convb_ref[...]
    new_hist = _keep_state(cfg, _seq_tails(xb, cfg), hist)
    hist_scr[srows, :] = new_hist
    tailx_ref[srows, :] = new_hist

    gate = _spatial_gating(v, ws_ref, bias_ref, cfg)
    half = D_B // 2
    xcb = xc.astype(BF16)
    g0 = jnp.dot(xcb[:, :half], wgate_ref[0], preferred_element_type=F32)
    g1 = jnp.dot(xcb[:, half:], wgate_ref[1], preferred_element_type=F32)
    between()

    a_out = u * gate
    r = jax.nn.sigmoid(jnp.concatenate([g0[:, :half], g1[:, :half]], axis=1) + ba_ref[...])
    ig = jax.nn.sigmoid(jnp.concatenate([g0[:, half:], g1[:, half:]], axis=1) + bx_ref[...])
    nl = -lam_ref[...]
    softplus = jnp.maximum(nl, 0.0) + jnp.log1p(jnp.exp(-jnp.abs(nl)))
    log_a = -LRU_C * r * softplus
    a = jnp.exp(log_a)
    bterm = jnp.sqrt(-jnp.tanh(log_a) * (a * a + 1.0)) * ig * xc
    h_prev = h_scr[srows, :]
    hseq = _linear_scan(a, bterm, h_prev, cfg)
    new_h = _keep_state(cfg, _seq_tails(hseq, cfg), h_prev)
    h_scr[srows, :] = new_h
    tailh_ref[srows, :] = new_h

    b_out = hseq * jax.nn.gelu(gb)
    mix = jnp.concatenate([a_out, b_out], axis=1).astype(BF16)
    return _mixer_out(x, mix, wout_ref, gpost_ref)


def _odd_mixer(cfg, x, z, refs, states, tile, between):
    convw_ref, wout_ref, gpost_ref = refs
    hist_scr, tail_ref = states
    srows = cfg.tile_state(tile)
    between()
    bg = z[:, 0:D_C]
    p = z[:, D_C:2 * D_C] * z[:, 2 * D_C:]
    hist = hist_scr[srows, :]
    y = _causal_conv(p, hist, convw_ref[...], cfg)
    new_hist = _keep_state(cfg, _seq_tails(p, cfg), hist)
    hist_scr[srows, :] = new_hist
    tail_ref[srows, :] = new_hist
    return _mixer_out(x, (bg * y).astype(BF16), wout_ref, gpost_ref)


def _paired(mixer, x, z_of, mlp_refs, mlp_x, write_mlp, up_first):
    gfpre_ref, up_ref, down_ref, gfpost_ref = mlp_refs
    if x is not None and mlp_x is not None:
        hn = _rmsnorm(mlp_x, gfpre_ref[...]).astype(BF16)
        hid = jnp.dot(hn, up_ref[:, 0:up_first], preferred_element_type=F32)
        z = z_of(x)
        if up_first < D_FF:
            rest = jnp.dot(hn, up_ref[:, up_first:], preferred_element_type=F32)
            hid = jnp.concatenate([hid, rest], axis=1)
    else:
        z = z_of(x) if x is not None else None
        hid = _mlp_up(mlp_x, gfpre_ref, up_ref) if mlp_x is not None else None

    def mlp_down():
        if mlp_x is not None:
            write_mlp(_mlp_down(mlp_x, hid, down_ref, gfpost_ref))

    if x is None:
        mlp_down()
        return None
    return mixer(x, z, mlp_down)


def _layer_body(cfg, mixer, z_of, mlp_refs, x_ref, xo_ref, x1_scr, up_first):
    rows = lambda t: slice(t * ROW_TILE, (t + 1) * ROW_TILE)

    def write_out(t):
        def write(val):
            xo_ref[rows(t), :] = val
        return write

    if not cfg.skewed:
        x1_prev = None
        for t in range(TILES_PER_STEP + 1):
            x = x_ref[rows(t), :] if t < TILES_PER_STEP else None
            x1 = _paired(functools.partial(mixer, t), x, z_of, mlp_refs, x1_prev,
                         write_out(t - 1) if t > 0 else None, up_first)
            x1_prev = x1
        return

    @pl.when(pl.program_id(0) == 0)
    def _():
        x1_scr[...] = jnp.zeros_like(x1_scr)

    for t in range(TILES_PER_STEP):
        x1_scr[rows(t), :] = _paired(functools.partial(mixer, t), x_ref[rows(t), :], z_of, mlp_refs,
                                     x1_scr[rows(t), :], write_out(t), up_first)


def _load_states_at_sequence_start(cfg, pairs):
    i = pl.program_id(0)

    @pl.when(jnp.logical_and(lax.rem(i, cfg.steps_per_seq) == 0, i < cfg.n_steps))
    def _():
        for scr, src in pairs:
            scr[...] = src[...]


def _cast_next_layer_weights(cast_in_refs, cast_out_refs):
    for src, dst in zip(cast_in_refs, cast_out_refs):
        dst[...] = src[...].astype(BF16)


BIG_NAMES = ("win", "wout", "up", "down")

ROW_GPRE, ROW_GPOST, ROW_GFPRE, ROW_GFPOST, ROW_CCONV = 0, 1, 2, 3, 4
N_VEC_ROWS = ROW_CCONV + C_CONV
ROW_VGAIN, ROW_CONVB, ROW_BA, ROW_BX, ROW_LAM, ROW_BCONV = 0, 1, 2, 3, 4, 5
N_EVEN_VEC_ROWS = ROW_BCONV + B_CONV


def _row(ref, r, n=1):
    return ref.at[r:r + n, :]


def _even_layer_kernel(cfg, n_cast, *refs):
    refs = list(refs)
    x_ref, vec_ref, evec_ref, wsb_ref, wgate_ref, win_ref, wout_ref, up_ref, down_ref, state_in_ref = refs[:10]
    cast_in_refs = refs[10:10 + n_cast]
    del refs[:10 + n_cast]
    xo_ref, tails_ref = refs[:2]
    del refs[:2]
    v_ref = refs.pop(0) if cfg.emit_v else None
    gpre_ref, gpost_ref = _row(vec_ref, ROW_GPRE), _row(vec_ref, ROW_GPOST)
    gfpre_ref, gfpost_ref = _row(vec_ref, ROW_GFPRE), _row(vec_ref, ROW_GFPOST)
    vgain_ref, convb_ref = _row(evec_ref, ROW_VGAIN), _row(evec_ref, ROW_CONVB)
    ba_ref, bx_ref, lam_ref = _row(evec_ref, ROW_BA), _row(evec_ref, ROW_BX), _row(evec_ref, ROW_LAM)
    convw_ref = _row(evec_ref, ROW_BCONV, B_CONV)
    ws_ref, bias_ref = wsb_ref.at[0], wsb_ref.at[1]
    hist_in_ref, h_in_ref = state_in_ref.at[:, 0:D_B], state_in_ref.at[:, D_B:2 * D_B]
    tailx_ref, tailh_ref = tails_ref.at[:, 0:D_B], tails_ref.at[:, D_B:2 * D_B]
    cast_out_refs = refs[:n_cast]
    del refs[:n_cast]
    hist_scr, h_scr = refs[:2]
    x1_scr = refs[2] if cfg.skewed else None

    mixer_refs = (vgain_ref, ws_ref, bias_ref, convw_ref, convb_ref, wgate_ref, ba_ref, bx_ref, lam_ref,
                  wout_ref, gpost_ref)
    states = (hist_scr, h_scr, tailx_ref, tailh_ref, v_ref)
    _load_states_at_sequence_start(cfg, [(hist_scr, hist_in_ref), (h_scr, h_in_ref)])

    def mixer(tile, x, z, between):
        return _even_mixer(cfg, x, z, mixer_refs, states, tile, between)

    _layer_body(cfg, mixer, lambda x: _mixer_in(x, gpre_ref, win_ref), (gfpre_ref, up_ref, down_ref, gfpost_ref),
                x_ref, xo_ref, x1_scr, up_first=D_FF // 2)
    _cast_next_layer_weights(cast_in_refs, cast_out_refs)


def _odd_layer_kernel(cfg, n_cast, *refs):
    refs = list(refs)
    x_ref, vec_ref, win_ref, wout_ref, up_ref, down_ref, hist_in_ref = refs[:7]
    cast_in_refs = refs[7:7 + n_cast]
    del refs[:7 + n_cast]
    gpre_ref, gpost_ref = _row(vec_ref, ROW_GPRE), _row(vec_ref, ROW_GPOST)
    gfpre_ref, gfpost_ref = _row(vec_ref, ROW_GFPRE), _row(vec_ref, ROW_GFPOST)
    convw_ref = _row(vec_ref, ROW_CCONV, C_CONV)
    xo_ref, tail_ref = refs[:2]
    del refs[:2]
    cast_out_refs = refs[:n_cast]
    del refs[:n_cast]
    hist_scr = refs[0]
    x1_scr = refs[1] if cfg.skewed else None

    _load_states_at_sequence_start(cfg, [(hist_scr, hist_in_ref)])

    def mixer(tile, x, z, between):
        return _odd_mixer(cfg, x, z, (convw_ref, wout_ref, gpost_ref), (hist_scr, tail_ref), tile, between)

    _layer_body(cfg, mixer, lambda x: _mixer_in(x, gpre_ref, win_ref), (gfpre_ref, up_ref, down_ref, gfpost_ref),
                x_ref, xo_ref, x1_scr, up_first=D_FF)
    _cast_next_layer_weights(cast_in_refs, cast_out_refs)


def _resident(arr, layer):
    if layer is None:
        zeros = (0,) * arr.ndim
        return pl.BlockSpec(arr.shape, lambda i: zeros, pipeline_mode=pl.Buffered(1))
    zeros = (0,) * (arr.ndim - 1)
    return pl.BlockSpec((None,) + arr.shape[1:], lambda i: (layer,) + zeros, pipeline_mode=pl.Buffered(1))


def _cast_specs(cfg, cast_src):
    in_specs, out_specs, out_shape = [], [], []
    for arr, k in cast_src:
        _, n_rows, n_cols = arr.shape
        chunk = n_rows // cfg.n_steps
        in_specs.append(pl.BlockSpec((None, chunk, n_cols), lambda i, _k=k: (_k, _mixer_step(cfg, i), 0)))
        out_specs.append(pl.BlockSpec((chunk, n_cols), lambda i: (_mixer_step(cfg, i), 0)))
        out_shape.append(jax.ShapeDtypeStruct((n_rows, n_cols), BF16))
    return in_specs, out_specs, out_shape


def _mixer_step(cfg, i):
    return jnp.minimum(i, cfg.n_steps - 1)


def _state_in_spec(cfg, width, k):
    return pl.BlockSpec((None, cfg.state_rows, width),
                        lambda i: (k, _mixer_step(cfg, i) // cfg.steps_per_seq, 0))


def _state_spec(cfg, width):
    return pl.BlockSpec((cfg.state_rows, width), lambda i: (_mixer_step(cfg, i) // cfg.steps_per_seq, 0))


def _row_in_spec(cfg, width):
    return pl.BlockSpec((STEP_ROWS, width), lambda i: (_mixer_step(cfg, i), 0))


def _row_out_spec(cfg, width):
    if cfg.skewed:
        return pl.BlockSpec((STEP_ROWS, width), lambda i: (jnp.maximum(i - 1, 0), 0))
    return pl.BlockSpec((STEP_ROWS, width), lambda i: (i, 0))


def _grid(cfg):
    return (cfg.n_steps + 1,) if cfg.skewed else (cfg.n_steps,)


def _skew_scratch(cfg):
    return [pltpu.VMEM((STEP_ROWS, D_MODEL), F32)] if cfg.skewed else []


def _compiler_params():
    return pltpu.CompilerParams(dimension_semantics=("arbitrary",), vmem_limit_bytes=VMEM_LIMIT_BYTES)


def _even_layer(cfg, x, state, w, big, layer, cast_src=()):
    rows = x.shape[0]
    e = layer // 2
    consts = [(w["vec"], layer), (w["evec"], e), (w["wsb"], e), (w["wgate"], e)] + [(big[n], None) for n in BIG_NAMES]
    cast_in, cast_out, cast_shape = _cast_specs(cfg, cast_src)
    in_specs = ([_row_in_spec(cfg, D_MODEL)] + [_resident(c, k) for c, k in consts]
                + [_state_in_spec(cfg, 2 * D_B, e)] + cast_in)
    out_shape = [jax.ShapeDtypeStruct((rows, D_MODEL), F32),
                 jax.ShapeDtypeStruct((state.shape[1], 2 * D_B), F32)]
    out_specs = [_row_out_spec(cfg, D_MODEL), _state_spec(cfg, 2 * D_B)]
    if cfg.emit_v:
        out_shape.append(jax.ShapeDtypeStruct((rows, D_A), F32))
        out_specs.append(_row_in_spec(cfg, D_A))
    scratch = ([pltpu.VMEM((cfg.state_rows, D_B), F32), pltpu.VMEM((cfg.state_rows, D_B), F32)]
               + _skew_scratch(cfg))
    return pl.pallas_call(
        functools.partial(_even_layer_kernel, cfg, len(cast_src)),
        grid=_grid(cfg),
        in_specs=in_specs,
        out_specs=out_specs + cast_out,
        out_shape=out_shape + cast_shape,
        scratch_shapes=scratch,
        compiler_params=_compiler_params(),
        name="even_layer",
    )(x, *[c for c, _ in consts], state, *[arr for arr, _ in cast_src])


def _odd_layer(cfg, x, hist, w, big, layer, cast_src=()):
    rows = x.shape[0]
    o = layer // 2
    consts = [(w["vec"], layer)] + [(big[n], None) for n in BIG_NAMES]
    cast_in, cast_out, cast_shape = _cast_specs(cfg, cast_src)
    in_specs = ([_row_in_spec(cfg, D_MODEL)] + [_resident(c, k) for c, k in consts]
                + [_state_in_spec(cfg, D_C, o)] + cast_in)
    out_shape = [jax.ShapeDtypeStruct((rows, D_MODEL), F32),
                 jax.ShapeDtypeStruct((hist.shape[1], D_C), F32)]
    out_specs = [_row_out_spec(cfg, D_MODEL), _state_spec(cfg, D_C)]
    scratch = [pltpu.VMEM((cfg.state_rows, D_C), F32)] + _skew_scratch(cfg)
    return pl.pallas_call(
        functools.partial(_odd_layer_kernel, cfg, len(cast_src)),
        grid=_grid(cfg),
        in_specs=in_specs,
        out_specs=out_specs + cast_out,
        out_shape=out_shape + cast_shape,
        scratch_shapes=scratch,
        compiler_params=_compiler_params(),
        name="odd_layer",
    )(x, *[c for c, _ in consts], hist, *[arr for arr, _ in cast_src])


def _block_diag_gates(wa, wx):
    heads_per_half = B_HEADS // 2
    eye = jnp.eye(heads_per_half, dtype=wa.dtype)

    def bd(w):
        return jnp.einsum("eqhij,hg->eqhigj", w, eye).reshape(w.shape[0], 2, D_B // 2, D_B // 2)

    split = lambda w: w.reshape(w.shape[0], 2, heads_per_half, B_HEAD_DIM, B_HEAD_DIM)
    return jnp.concatenate([bd(split(wa)), bd(split(wx))], axis=-1).astype(BF16)


def _pad_state(rows):
    layers, n, r, c = rows.shape
    return jnp.pad(rows, ((0, 0), (0, 0), (SUBLANES - r, 0), (0, 0))).reshape(layers, n * SUBLANES, c)


def _small_weights(p):
    depth = p["norm_mix_pre"].shape[0]
    gains = jnp.stack([p["norm_mix_pre"], p["norm_mix_post"], p["norm_ffn_pre"], p["norm_ffn_post"]], axis=1)
    cconv = jnp.zeros((depth, C_CONV, D_C), F32).at[1::2].set(p["c_conv_w"])
    row = lambda v: v[:, None, :]
    evec = jnp.concatenate([row(p["a_v_gain"]), row(p["b_conv_b"]), row(p["b_ba"]), row(p["b_bx"]),
                            row(p["b_lambda"]), p["b_conv_w"]], axis=1)
    a_b_s = p["a_b_s"]
    bias = jnp.broadcast_to(a_b_s[..., None], a_b_s.shape + (A_HEAD_DIM,))
    return dict(vec=jnp.concatenate([gains, cconv], axis=1), evec=evec, wsb=jnp.stack([p["a_w_s"], bias], axis=1),
                wgate=_block_diag_gates(p["b_wa"], p["b_wx"]))


def _big_f32(p, layer):
    k = layer // 2
    if layer % 2 == 0:
        return [(p["w_in_even"], k), (p["w_out_even"], k), (p["mlp_up"], layer), (p["mlp_down"], layer)]
    return [(p["c_w_in"], k), (p["c_w_out"], k), (p["mlp_up"], layer), (p["mlp_down"], layer)]


def _trunk(x, b_state, c_conv, small, big, cfg, p=None):
    tails_b, tails_h, tails_c, v_rows = [], [], [], []
    for l in range(DEPTH):
        cast_src = _big_f32(p, l + 1) if p is not None and l + 1 < DEPTH else ()
        if l % 2 == 0:
            outs = _even_layer(cfg, x, b_state, small, big[l], l, cast_src)
            n_fixed = 3 if cfg.emit_v else 2
            tails_b.append(outs[1][:, :D_B])
            tails_h.append(outs[1][:, D_B:])
            if cfg.emit_v:
                v_rows.append(outs[2])
        else:
            outs = _odd_layer(cfg, x, c_conv, small, big[l], l, cast_src)
            n_fixed = 2
            tails_c.append(outs[1])
        x = outs[0]
        if cast_src:
            big[l + 1] = dict(zip(BIG_NAMES, outs[n_fixed:]))
    return x, tails_b, tails_h, tails_c, v_rows


def kernel(x_prompt, x_sample, cache_b_conv, state_b_h, cache_c_conv, norm_mix_pre, norm_mix_post, norm_ffn_pre, norm_ffn_post, w_in_even, a_v_gain, a_w_s, a_b_s, b_conv_w, b_conv_b, b_wa, b_ba, b_wx, b_bx, b_lambda, w_out_even, c_w_in, c_conv_w, c_w_out, mlp_up, mlp_down):
    batch, seq, _ = x_prompt.shape
    dec_batch, dec_seq, _ = x_sample.shape
    n_even, n_odd = w_in_even.shape[0], c_w_in.shape[0]
    p = dict(norm_mix_pre=norm_mix_pre, norm_mix_post=norm_mix_post, norm_ffn_pre=norm_ffn_pre,
             norm_ffn_post=norm_ffn_post, w_in_even=w_in_even, a_v_gain=a_v_gain, a_w_s=a_w_s, a_b_s=a_b_s,
             b_conv_w=b_conv_w, b_conv_b=b_conv_b, b_wa=b_wa, b_wx=b_wx, b_ba=b_ba, b_bx=b_bx,
             b_lambda=b_lambda, w_out_even=w_out_even, c_w_in=c_w_in, c_conv_w=c_conv_w, c_w_out=c_w_out,
             mlp_up=mlp_up, mlp_down=mlp_down)
    small = _small_weights(p)
    big = {0: {name: arr[k].astype(BF16) for name, (arr, k) in zip(BIG_NAMES, _big_f32(p, 0))}}

    cfg_p = TileCfg(n_seq=1, seq_rows=ROW_TILE, steps_per_seq=seq // STEP_ROWS, gate_chunk=A_CHUNK,
                    emit_v=False, n_steps=batch * seq // STEP_ROWS, skewed=True)
    zb = jnp.zeros((n_even, batch * SUBLANES, 2 * D_B), F32)
    zc = jnp.zeros((n_odd, batch * SUBLANES, D_C), F32)
    y_p, tb_p, th_p, tc_p, _ = _trunk(x_prompt.reshape(batch * seq, D_MODEL), zb, zc, small, big, cfg_p, p)

    cfg_s = TileCfg(n_seq=ROW_TILE // dec_seq, seq_rows=dec_seq, steps_per_seq=1, gate_chunk=dec_seq,
                    emit_v=True, n_steps=dec_batch * dec_seq // STEP_ROWS, skewed=False)
    sb = jnp.concatenate([_pad_state(cache_b_conv), _pad_state(state_b_h[:, :, None, :])], axis=-1)
    sc = _pad_state(cache_c_conv)
    y_s, tb_s, th_s, tc_s, v_s = _trunk(x_sample.reshape(dec_batch * dec_seq, D_MODEL), sb, sc, small, big, cfg_s)

    def tails(ts, n, keep):
        t = jnp.stack(ts).reshape(len(ts), n, SUBLANES, -1)
        return t[:, :, SUBLANES - keep:, :]

    return (y_p.reshape(batch, seq, D_MODEL),
            y_s.reshape(dec_batch, dec_seq, D_MODEL),
            jnp.stack(v_s).reshape(n_even, dec_batch, dec_seq, D_A),
            tails(tb_p, batch, B_CONV - 1),
            tails(th_p, batch, 1)[:, :, 0, :],
            tails(tc_p, batch, C_CONV - 1),
            tails(tb_s, dec_batch, B_CONV - 1),
            tails(th_s, dec_batch, 1)[:, :, 0, :],
            tails(tc_s, dec_batch, C_CONV - 1))
```

```python
import functools
from typing import NamedTuple

import jax
import jax.numpy as jnp
from jax import lax
from jax.experimental import pallas as pl
from jax.experimental.pallas import tpu as pltpu

D_MODEL = 1024
DEPTH = 4
CHUNK = 64
A_CHUNK = 128
D_A = D_MODEL // 2
A_HEADS = 4
A_HEAD_DIM = D_A // A_HEADS
D_B = D_MODEL // 2
B_HEADS = 8
B_HEAD_DIM = D_B // B_HEADS
B_CONV = 4
LRU_C = 8.0
D_C = D_MODEL
C_CONV = 3
D_FF = 4 * D_MODEL
EPS = 1e-6

SUBLANES = 8
V7X_VMEM_BYTES = 64 * 1024 * 1024
VMEM_LIMIT_BYTES = V7X_VMEM_BYTES - 8 * 1024 * 1024
ROW_TILE = 512
PROMPT_TILES_PER_STEP = 1
SAMPLE_TILES_PER_STEP = 1

F32 = jnp.float32
BF16 = jnp.bfloat16


class TileCfg(NamedTuple):
    n_seq: int
    seq_rows: int
    steps_per_seq: int
    gate_chunk: int
    emit_v: bool
    n_steps: int
    skewed: bool
    tiles: int

    @property
    def step_rows(self):
        return self.tiles * ROW_TILE

    @property
    def state_rows(self):
        groups = self.tiles if self.n_seq > 1 else 1
        return groups * self.n_seq * SUBLANES

    def tile_state(self, t):
        if self.n_seq == 1:
            return slice(0, SUBLANES)
        return slice(t * self.n_seq * SUBLANES, (t + 1) * self.n_seq * SUBLANES)


def _rmsnorm(x, g):
    ms = jnp.mean(x * x, axis=-1, keepdims=True)
    return x * lax.rsqrt(ms + EPS) * g


def _shift_rows(x, hist, k, cfg):
    if k == 0:
        return x
    rolled = pltpu.roll(x, k, 0)
    row = lax.broadcasted_iota(jnp.int32, (SUBLANES, x.shape[1]), 0)
    pieces = []
    for s in range(cfg.n_seq):
        lo = s * cfg.seq_rows
        h = pltpu.roll(hist[s * SUBLANES:(s + 1) * SUBLANES], k, 0)
        pieces.append(jnp.where(row < k, h, rolled[lo:lo + SUBLANES]))
        pieces.append(rolled[lo + SUBLANES:lo + cfg.seq_rows])
    return jnp.concatenate(pieces, axis=0)


def _causal_conv(x, hist, w, cfg):
    width = w.shape[0]
    y = w[0:1] * _shift_rows(x, hist, width - 1, cfg)
    for k in range(1, width):
        y = y + w[k:k + 1] * _shift_rows(x, hist, width - 1 - k, cfg)
    return y


def _seq_tails(x, cfg):
    pieces = [x[(s + 1) * cfg.seq_rows - SUBLANES:(s + 1) * cfg.seq_rows] for s in range(cfg.n_seq)]
    return pieces[0] if len(pieces) == 1 else jnp.concatenate(pieces, axis=0)


def _linear_scan(a, b, h_prev, cfg):
    row = lax.broadcasted_iota(jnp.int32, a.shape, 0) & (SUBLANES - 1)
    d = 1
    while d < SUBLANES:
        keep = row >= d
        a_sh = pltpu.roll(a, d, 0)
        b_sh = pltpu.roll(b, d, 0)
        b = jnp.where(keep, a * b_sh + b, b)
        a = jnp.where(keep, a * a_sh, a)
        d *= 2
    groups_per_seq = cfg.seq_rows // SUBLANES
    out = []
    for s in range(cfg.n_seq):
        carry = h_prev[s * SUBLANES + SUBLANES - 1:(s + 1) * SUBLANES]
        for g in range(groups_per_seq):
            lo = s * cfg.seq_rows + g * SUBLANES
            hg = b[lo:lo + SUBLANES] + a[lo:lo + SUBLANES] * carry
            out.append(hg)
            carry = hg[SUBLANES - 1:SUBLANES]
    return jnp.concatenate(out, axis=0)


def _spatial_gating(v, ws_ref, bias_ref, cfg):
    lc = cfg.gate_chunk
    n_chunks = v.shape[0] // lc
    vb = v.astype(BF16)
    pi = lax.shift_right_logical(lax.broadcasted_iota(jnp.int32, (lc, lc), 0), 6)
    pj = lax.shift_right_logical(lax.broadcasted_iota(jnp.int32, (lc, lc), 1), 6)
    mask = pj <= pi
    heads = []
    for h in range(A_HEADS):
        lanes = slice(h * A_HEAD_DIM, (h + 1) * A_HEAD_DIM)
        w = jnp.where(mask, ws_ref[h, 0:lc, 0:lc], 0.0).astype(BF16)
        vh = jnp.concatenate([vb[c * lc:(c + 1) * lc, lanes] for c in range(n_chunks)], axis=1)
        sh = jnp.dot(w, vh, preferred_element_type=F32)
        bias = bias_ref[h, 0:lc, :]
        heads.append(jnp.concatenate(
            [sh[:, c * A_HEAD_DIM:(c + 1) * A_HEAD_DIM] + bias for c in range(n_chunks)], axis=0))
    return jnp.concatenate(heads, axis=1)


def _mixer_in(x, gpre_ref, win_ref):
    hn = _rmsnorm(x, gpre_ref[...]).astype(BF16)
    return jnp.dot(hn, win_ref[...], preferred_element_type=F32)


def _mixer_out(x, mix, wout_ref, gpost_ref):
    m = jnp.dot(mix, wout_ref[...], preferred_element_type=F32)
    return x + _rmsnorm(m, gpost_ref[...])


def _mlp_up(x, gpre_ref, up_ref):
    hn = _rmsnorm(x, gpre_ref[...]).astype(BF16)
    return jnp.dot(hn, up_ref[...], preferred_element_type=F32)


def _mlp_down(x, hid, down_ref, gpost_ref):
    act = jnp.square(jnp.maximum(hid, 0.0)).astype(BF16)
    f = jnp.dot(act, down_ref[...], preferred_element_type=F32)
    return x + _rmsnorm(f, gpost_ref[...])


def _keep_state(cfg, new, old):
    return jnp.where(pl.program_id(0) < cfg.n_steps, new, old) if cfg.skewed else new


def _even_mixer(cfg, x, z, refs, states, tile, between):
    (vgain_ref, ws_ref, bias_ref, convw_ref, convb_ref, wgate_ref, ba_ref, bx_ref, lam_ref, wout_ref,
     gpost_ref) = refs
    hist_scr, h_scr, tailx_ref, tailh_ref, v_ref = states
    srows = cfg.tile_state(tile)
    u = jax.nn.gelu(z[:, 0:D_A])
    v = _rmsnorm(jax.nn.gelu(z[:, D_A:2 * D_A]), vgain_ref[...])
    xb = z[:, 2 * D_A:2 * D_A + D_B]
    gb = z[:, 2 * D_A + D_B:]
    if v_ref is not None:
        v_ref[tile * ROW_TILE:(tile + 1) * ROW_TILE, :] = v
    hist = hist_scr[srows, :]
    xc = _causal_conv(xb, hist, convw_ref[...], cfg) + convb_ref[...]
    new_hist = _keep_state(cfg, _seq_tails(xb, cfg), hist)
    hist_scr[srows, :] = new_hist
    tailx_ref[srows, :] = new_hist

    gate = _spatial_gating(v, ws_ref, bias_ref, cfg)
    half = D_B // 2
    xcb = xc.astype(BF16)
    g0 = jnp.dot(xcb[:, :half], wgate_ref[0], preferred_element_type=F32)
    g1 = jnp.dot(xcb[:, half:], wgate_ref[1], preferred_element_type=F32)
    between()

    a_out = u * gate
    r = jax.nn.sigmoid(jnp.concatenate([g0[:, :half], g1[:, :half]], axis=1) + ba_ref[...])
    ig = jax.nn.sigmoid(jnp.concatenate([g0[:, half:], g1[:, half:]], axis=1) + bx_ref[...])
    nl = -lam_ref[...]
    softplus = jnp.maximum(nl, 0.0) + jnp.log1p(jnp.exp(-jnp.abs(nl)))
    log_a = -LRU_C * r * softplus
    a = jnp.exp(log_a)
    bterm = jnp.sqrt(-jnp.tanh(log_a) * (a * a + 1.0)) * ig * xc
    h_prev = h_scr[srows, :]
    hseq = _linear_scan(a, bterm, h_prev, cfg)
    new_h = _keep_state(cfg, _seq_tails(hseq, cfg), h_prev)
    h_scr[srows, :] = new_h
    tailh_ref[srows, :] = new_h

    b_out = hseq * jax.nn.gelu(gb)
    mix = jnp.concatenate([a_out, b_out], axis=1).astype(BF16)
    return _mixer_out(x, mix, wout_ref, gpost_ref)


def _odd_mixer(cfg, x, z, refs, states, tile, between):
    convw_ref, wout_ref, gpost_ref = refs
    hist_scr, tail_ref = states
    srows = cfg.tile_state(tile)
    between()
    bg = z[:, 0:D_C]
    p = z[:, D_C:2 * D_C] * z[:, 2 * D_C:]
    hist = hist_scr[srows, :]
    y = _causal_conv(p, hist, convw_ref[...], cfg)
    new_hist = _keep_state(cfg, _seq_tails(p, cfg), hist)
    hist_scr[srows, :] = new_hist
    tail_ref[srows, :] = new_hist
    return _mixer_out(x, (bg * y).astype(BF16), wout_ref, gpost_ref)


def _paired(mixer, x, z_of, mlp_refs, mlp_x, write_mlp, up_first):
    gfpre_ref, up_ref, down_ref, gfpost_ref = mlp_refs
    if x is not None and mlp_x is not None:
        hn = _rmsnorm(mlp_x, gfpre_ref[...]).astype(BF16)
        hid = jnp.dot(hn, up_ref[:, 0:up_first], preferred_element_type=F32)
        z = z_of(x)
        if up_first < D_FF:
            rest = jnp.dot(hn, up_ref[:, up_first:], preferred_element_type=F32)
            hid = jnp.concatenate([hid, rest], axis=1)
    else:
        z = z_of(x) if x is not None else None
        hid = _mlp_up(mlp_x, gfpre_ref, up_ref) if mlp_x is not None else None

    def mlp_down():
        if mlp_x is not None:
            write_mlp(_mlp_down(mlp_x, hid, down_ref, gfpost_ref))

    if x is None:
        mlp_down()
        return None
    return mixer(x, z, mlp_down)


def _layer_body(cfg, mixer, z_of, mlp_refs, x_ref, xo_ref, x1_scr, up_first):
    rows = lambda t: slice(t * ROW_TILE, (t + 1) * ROW_TILE)

    def write_out(t):
        def write(val):
            xo_ref[rows(t), :] = val
        return write

    if not cfg.skewed:
        x1_prev = None
        for t in range(cfg.tiles + 1):
            x = x_ref[rows(t), :] if t < cfg.tiles else None
            x1 = _paired(functools.partial(mixer, t), x, z_of, mlp_refs, x1_prev,
                         write_out(t - 1) if t > 0 else None, up_first)
            x1_prev = x1
        return

    i = pl.program_id(0)

    def step(do_mixer, do_mlp):
        for t in range(cfg.tiles):
            x = x_ref[rows(t), :] if do_mixer else None
            mlp_x = x1_scr[rows(t), :] if do_mlp else None
            x1 = _paired(functools.partial(mixer, t), x, z_of, mlp_refs, mlp_x, write_out(t), up_first)
            if do_mixer:
                x1_scr[rows(t), :] = x1

    pl.when(i == 0)(lambda: step(True, False))
    pl.when(jnp.logical_and(i > 0, i < cfg.n_steps))(lambda: step(True, True))
    pl.when(i == cfg.n_steps)(lambda: step(False, True))


def _load_states_at_sequence_start(cfg, pairs):
    i = pl.program_id(0)

    @pl.when(jnp.logical_and(lax.rem(i, cfg.steps_per_seq) == 0, i < cfg.n_steps))
    def _():
        for scr, src in pairs:
            scr[...] = src[...]


def _cast_next_layer_weights(cast_in_refs, cast_out_refs):
    for src, dst in zip(cast_in_refs, cast_out_refs):
        dst[...] = src[...].astype(BF16)


BIG_NAMES = ("win", "wout", "up", "down")

ROW_GPRE, ROW_GPOST, ROW_GFPRE, ROW_GFPOST, ROW_CCONV = 0, 1, 2, 3, 4
N_VEC_ROWS = ROW_CCONV + C_CONV
ROW_VGAIN, ROW_CONVB, ROW_BA, ROW_BX, ROW_LAM, ROW_BCONV = 0, 1, 2, 3, 4, 5
N_EVEN_VEC_ROWS = ROW_BCONV + B_CONV


def _row(ref, r, n=1):
    return ref.at[r:r + n, :]


def _even_layer_kernel(cfg, n_cast, *refs):
    refs = list(refs)
    x_ref, vec_ref, evec_ref, wsb_ref, wgate_ref, win_ref, wout_ref, up_ref, down_ref, state_in_ref = refs[:10]
    cast_in_refs = refs[10:10 + n_cast]
    del refs[:10 + n_cast]
    xo_ref, tails_ref = refs[:2]
    del refs[:2]
    v_ref = refs.pop(0) if cfg.emit_v else None
    gpre_ref, gpost_ref = _row(vec_ref, ROW_GPRE), _row(vec_ref, ROW_GPOST)
    gfpre_ref, gfpost_ref = _row(vec_ref, ROW_GFPRE), _row(vec_ref, ROW_GFPOST)
    vgain_ref, convb_ref = _row(evec_ref, ROW_VGAIN), _row(evec_ref, ROW_CONVB)
    ba_ref, bx_ref, lam_ref = _row(evec_ref, ROW_BA), _row(evec_ref, ROW_BX), _row(evec_ref, ROW_LAM)
    convw_ref = _row(evec_ref, ROW_BCONV, B_CONV)
    ws_ref, bias_ref = wsb_ref.at[0], wsb_ref.at[1]
    hist_in_ref, h_in_ref = state_in_ref.at[:, 0:D_B], state_in_ref.at[:, D_B:2 * D_B]
    tailx_ref, tailh_ref = tails_ref.at[:, 0:D_B], tails_ref.at[:, D_B:2 * D_B]
    cast_out_refs = refs[:n_cast]
    del refs[:n_cast]
    hist_scr, h_scr = refs[:2]
    x1_scr = refs[2] if cfg.skewed else None

    mixer_refs = (vgain_ref, ws_ref, bias_ref, convw_ref, convb_ref, wgate_ref, ba_ref, bx_ref, lam_ref,
                  wout_ref, gpost_ref)
    states = (hist_scr, h_scr, tailx_ref, tailh_ref, v_ref)
    _load_states_at_sequence_start(cfg, [(hist_scr, hist_in_ref), (h_scr, h_in_ref)])

    def mixer(tile, x, z, between):
        return _even_mixer(cfg, x, z, mixer_refs, states, tile, between)

    _layer_body(cfg, mixer, lambda x: _mixer_in(x, gpre_ref, win_ref), (gfpre_ref, up_ref, down_ref, gfpost_ref),
                x_ref, xo_ref, x1_scr, up_first=D_FF // 2)
    _cast_next_layer_weights(cast_in_refs, cast_out_refs)


def _odd_layer_kernel(cfg, n_cast, *refs):
    refs = list(refs)
    x_ref, vec_ref, win_ref, wout_ref, up_ref, down_ref, hist_in_ref = refs[:7]
    cast_in_refs = refs[7:7 + n_cast]
    del refs[:7 + n_cast]
    gpre_ref, gpost_ref = _row(vec_ref, ROW_GPRE), _row(vec_ref, ROW_GPOST)
    gfpre_ref, gfpost_ref = _row(vec_ref, ROW_GFPRE), _row(vec_ref, ROW_GFPOST)
    convw_ref = _row(vec_ref, ROW_CCONV, C_CONV)
    xo_ref, tail_ref = refs[:2]
    del refs[:2]
    cast_out_refs = refs[:n_cast]
    del refs[:n_cast]
    hist_scr = refs[0]
    x1_scr = refs[1] if cfg.skewed else None

    _load_states_at_sequence_start(cfg, [(hist_scr, hist_in_ref)])

    def mixer(tile, x, z, between):
        return _odd_mixer(cfg, x, z, (convw_ref, wout_ref, gpost_ref), (hist_scr, tail_ref), tile, between)

    _layer_body(cfg, mixer, lambda x: _mixer_in(x, gpre_ref, win_ref), (gfpre_ref, up_ref, down_ref, gfpost_ref),
                x_ref, xo_ref, x1_scr, up_first=D_FF)
    _cast_next_layer_weights(cast_in_refs, cast_out_refs)


def _resident(arr, layer):
    if layer is None:
        zeros = (0,) * arr.ndim
        return pl.BlockSpec(arr.shape, lambda i: zeros, pipeline_mode=pl.Buffered(1))
    zeros = (0,) * (arr.ndim - 1)
    return pl.BlockSpec((None,) + arr.shape[1:], lambda i: (layer,) + zeros, pipeline_mode=pl.Buffered(1))


def _cast_specs(cfg, cast_src):
    in_specs, out_specs, out_shape = [], [], []
    for arr, k in cast_src:
        _, n_rows, n_cols = arr.shape
        chunk = n_rows // cfg.n_steps
        in_specs.append(pl.BlockSpec((None, chunk, n_cols), lambda i, _k=k: (_k, _mixer_step(cfg, i), 0)))
        out_specs.append(pl.BlockSpec((chunk, n_cols), lambda i: (_mixer_step(cfg, i), 0)))
        out_shape.append(jax.ShapeDtypeStruct((n_rows, n_cols), BF16))
    return in_specs, out_specs, out_shape


def _mixer_step(cfg, i):
    return jnp.minimum(i, cfg.n_steps - 1)


def _state_in_spec(cfg, width, k):
    return pl.BlockSpec((None, cfg.state_rows, width),
                        lambda i: (k, _mixer_step(cfg, i) // cfg.steps_per_seq, 0))


def _state_spec(cfg, width):
    return pl.BlockSpec((cfg.state_rows, width), lambda i: (_mixer_step(cfg, i) // cfg.steps_per_seq, 0))


def _row_in_spec(cfg, width):
    return pl.BlockSpec((cfg.step_rows, width), lambda i: (_mixer_step(cfg, i), 0))


def _row_out_spec(cfg, width):
    if cfg.skewed:
        return pl.BlockSpec((cfg.step_rows, width), lambda i: (jnp.maximum(i - 1, 0), 0))
    return pl.BlockSpec((cfg.step_rows, width), lambda i: (i, 0))


def _grid(cfg):
    return (cfg.n_steps + 1,) if cfg.skewed else (cfg.n_steps,)


def _skew_scratch(cfg):
    return [pltpu.VMEM((cfg.step_rows, D_MODEL), F32)] if cfg.skewed else []


def _compiler_params():
    return pltpu.CompilerParams(dimension_semantics=("arbitrary",), vmem_limit_bytes=VMEM_LIMIT_BYTES)


def _even_layer(cfg, x, state, w, big, layer, cast_src=()):
    rows = x.shape[0]
    e = layer // 2
    consts = [(w["vec"], layer), (w["evec"], e), (w["wsb"], e), (w["wgate"], e)] + [(big[n], None) for n in BIG_NAMES]
    cast_in, cast_out, cast_shape = _cast_specs(cfg, cast_src)
    in_specs = ([_row_in_spec(cfg, D_MODEL)] + [_resident(c, k) for c, k in consts]
                + [_state_in_spec(cfg, 2 * D_B, e)] + cast_in)
    out_shape = [jax.ShapeDtypeStruct((rows, D_MODEL), F32),
                 jax.ShapeDtypeStruct((state.shape[1], 2 * D_B), F32)]
    out_specs = [_row_out_spec(cfg, D_MODEL), _state_spec(cfg, 2 * D_B)]
    if cfg.emit_v:
        out_shape.append(jax.ShapeDtypeStruct((rows, D_A), F32))
        out_specs.append(_row_in_spec(cfg, D_A))
    scratch = ([pltpu.VMEM((cfg.state_rows, D_B), F32), pltpu.VMEM((cfg.state_rows, D_B), F32)]
               + _skew_scratch(cfg))
    return pl.pallas_call(
        functools.partial(_even_layer_kernel, cfg, len(cast_src)),
        grid=_grid(cfg),
        in_specs=in_specs,
        out_specs=out_specs + cast_out,
        out_shape=out_shape + cast_shape,
        scratch_shapes=scratch,
        compiler_params=_compiler_params(),
        name="even_layer",
    )(x, *[c for c, _ in consts], state, *[arr for arr, _ in cast_src])


def _odd_layer(cfg, x, hist, w, big, layer, cast_src=()):
    rows = x.shape[0]
    o = layer // 2
    consts = [(w["vec"], layer)] + [(big[n], None) for n in BIG_NAMES]
    cast_in, cast_out, cast_shape = _cast_specs(cfg, cast_src)
    in_specs = ([_row_in_spec(cfg, D_MODEL)] + [_resident(c, k) for c, k in consts]
                + [_state_in_spec(cfg, D_C, o)] + cast_in)
    out_shape = [jax.ShapeDtypeStruct((rows, D_MODEL), F32),
                 jax.ShapeDtypeStruct((hist.shape[1], D_C), F32)]
    out_specs = [_row_out_spec(cfg, D_MODEL), _state_spec(cfg, D_C)]
    scratch = [pltpu.VMEM((cfg.state_rows, D_C), F32)] + _skew_scratch(cfg)
    return pl.pallas_call(
        functools.partial(_odd_layer_kernel, cfg, len(cast_src)),
        grid=_grid(cfg),
        in_specs=in_specs,
        out_specs=out_specs + cast_out,
        out_shape=out_shape + cast_shape,
        scratch_shapes=scratch,
        compiler_params=_compiler_params(),
        name="odd_layer",
    )(x, *[c for c, _ in consts], hist, *[arr for arr, _ in cast_src])


def _block_diag_gates(wa, wx):
    heads_per_half = B_HEADS // 2
    eye = jnp.eye(heads_per_half, dtype=wa.dtype)

    def bd(w):
        return jnp.einsum("eqhij,hg->eqhigj", w, eye).reshape(w.shape[0], 2, D_B // 2, D_B // 2)

    split = lambda w: w.reshape(w.shape[0], 2, heads_per_half, B_HEAD_DIM, B_HEAD_DIM)
    return jnp.concatenate([bd(split(wa)), bd(split(wx))], axis=-1).astype(BF16)


def _pad_state(rows):
    layers, n, r, c = rows.shape
    return jnp.pad(rows, ((0, 0), (0, 0), (SUBLANES - r, 0), (0, 0))).reshape(layers, n * SUBLANES, c)


def _small_weights(p):
    depth = p["norm_mix_pre"].shape[0]
    gains = jnp.stack([p["norm_mix_pre"], p["norm_mix_post"], p["norm_ffn_pre"], p["norm_ffn_post"]], axis=1)
    cconv = jnp.zeros((depth, C_CONV, D_C), F32).at[1::2].set(p["c_conv_w"])
    row = lambda v: v[:, None, :]
    evec = jnp.concatenate([row(p["a_v_gain"]), row(p["b_conv_b"]), row(p["b_ba"]), row(p["b_bx"]),
                            row(p["b_lambda"]), p["b_conv_w"]], axis=1)
    a_b_s = p["a_b_s"]
    bias = jnp.broadcast_to(a_b_s[..., None], a_b_s.shape + (A_HEAD_DIM,))
    vec = jnp.concatenate([gains, cconv], axis=1)
    assert vec.shape[1] == N_VEC_ROWS and evec.shape[1] == N_EVEN_VEC_ROWS
    return dict(vec=vec, evec=evec, wsb=jnp.stack([p["a_w_s"], bias], axis=1),
                wgate=_block_diag_gates(p["b_wa"], p["b_wx"]))


def _big_f32(p, layer):
    k = layer // 2
    if layer % 2 == 0:
        return [(p["w_in_even"], k), (p["w_out_even"], k), (p["mlp_up"], layer), (p["mlp_down"], layer)]
    return [(p["c_w_in"], k), (p["c_w_out"], k), (p["mlp_up"], layer), (p["mlp_down"], layer)]


def _trunk(x, b_state, c_conv, small, big, cfg, p=None):
    tails_b, tails_h, tails_c, v_rows = [], [], [], []
    for l in range(DEPTH):
        cast_src = _big_f32(p, l + 1) if p is not None and l + 1 < DEPTH else ()
        if l % 2 == 0:
            outs = _even_layer(cfg, x, b_state, small, big[l], l, cast_src)
            n_fixed = 3 if cfg.emit_v else 2
            tails_b.append(outs[1][:, :D_B])
            tails_h.append(outs[1][:, D_B:])
            if cfg.emit_v:
                v_rows.append(outs[2])
        else:
            outs = _odd_layer(cfg, x, c_conv, small, big[l], l, cast_src)
            n_fixed = 2
            tails_c.append(outs[1])
        x = outs[0]
        if cast_src:
            big[l + 1] = dict(zip(BIG_NAMES, outs[n_fixed:]))
    return x, tails_b, tails_h, tails_c, v_rows


def kernel(x_prompt, x_sample, cache_b_conv, state_b_h, cache_c_conv, norm_mix_pre, norm_mix_post, norm_ffn_pre, norm_ffn_post, w_in_even, a_v_gain, a_w_s, a_b_s, b_conv_w, b_conv_b, b_wa, b_ba, b_wx, b_bx, b_lambda, w_out_even, c_w_in, c_conv_w, c_w_out, mlp_up, mlp_down):
    batch, seq, _ = x_prompt.shape
    dec_batch, dec_seq, _ = x_sample.shape
    n_even, n_odd = w_in_even.shape[0], c_w_in.shape[0]
    prompt_rows, sample_rows = PROMPT_TILES_PER_STEP * ROW_TILE, SAMPLE_TILES_PER_STEP * ROW_TILE
    assert seq % prompt_rows == 0 and ROW_TILE % A_CHUNK == 0
    assert dec_seq <= A_CHUNK and dec_seq % SUBLANES == 0 and ROW_TILE % dec_seq == 0
    assert (dec_batch * dec_seq) % sample_rows == 0
    assert D_MODEL % (batch * seq // prompt_rows * 2 * SUBLANES) == 0
    p = dict(norm_mix_pre=norm_mix_pre, norm_mix_post=norm_mix_post, norm_ffn_pre=norm_ffn_pre,
             norm_ffn_post=norm_ffn_post, w_in_even=w_in_even, a_v_gain=a_v_gain, a_w_s=a_w_s, a_b_s=a_b_s,
             b_conv_w=b_conv_w, b_conv_b=b_conv_b, b_wa=b_wa, b_wx=b_wx, b_ba=b_ba, b_bx=b_bx,
             b_lambda=b_lambda, w_out_even=w_out_even, c_w_in=c_w_in, c_conv_w=c_conv_w, c_w_out=c_w_out,
             mlp_up=mlp_up, mlp_down=mlp_down)
    small = _small_weights(p)
    big = {0: {name: arr[k].astype(BF16) for name, (arr, k) in zip(BIG_NAMES, _big_f32(p, 0))}}

    cfg_p = TileCfg(n_seq=1, seq_rows=ROW_TILE, steps_per_seq=seq // prompt_rows, gate_chunk=A_CHUNK,
                    emit_v=False, n_steps=batch * seq // prompt_rows, skewed=True, tiles=PROMPT_TILES_PER_STEP)
    zb = jnp.zeros((n_even, batch * SUBLANES, 2 * D_B), F32)
    zc = jnp.zeros((n_odd, batch * SUBLANES, D_C), F32)
    y_p, tb_p, th_p, tc_p, _ = _trunk(x_prompt.reshape(batch * seq, D_MODEL), zb, zc, small, big, cfg_p, p)

    cfg_s = TileCfg(n_seq=ROW_TILE // dec_seq, seq_rows=dec_seq, steps_per_seq=1, gate_chunk=dec_seq,
                    emit_v=True, n_steps=dec_batch * dec_seq // sample_rows, skewed=False,
                    tiles=SAMPLE_TILES_PER_STEP)
    sb = jnp.concatenate([_pad_state(cache_b_conv), _pad_state(state_b_h[:, :, None, :])], axis=-1)
    sc = _pad_state(cache_c_conv)
    y_s, tb_s, th_s, tc_s, v_s = _trunk(x_sample.reshape(dec_batch * dec_seq, D_MODEL), sb, sc, small, big, cfg_s)

    def tails(ts, n, keep):
        t = jnp.stack(ts).reshape(len(ts), n, SUBLANES, -1)
        return t[:, :, SUBLANES - keep:, :]

    return (y_p.reshape(batch, seq, D_MODEL),
            y_s.reshape(dec_batch, dec_seq, D_MODEL),
            jnp.stack(v_s).reshape(n_even, dec_batch, dec_seq, D_A),
            tails(tb_p, batch, B_CONV - 1),
            tails(th_p, batch, 1)[:, :, 0, :],
            tails(tc_p, batch, C_CONV - 1),
            tails(tb_s, dec_batch, B_CONV - 1),
            tails(th_s, dec_batch, 1)[:, :, 0, :],
            tails(tc_s, dec_batch, C_CONV - 1))
```

```python
import functools
from typing import NamedTuple

import jax
import jax.numpy as jnp
from jax import lax
from jax.experimental import pallas as pl
from jax.experimental.pallas import tpu as pltpu

D_MODEL = 1024
DEPTH = 4
CHUNK = 64
A_CHUNK = 128
D_A = D_MODEL // 2
A_HEADS = 4
A_HEAD_DIM = D_A // A_HEADS
D_B = D_MODEL // 2
B_HEADS = 8
B_HEAD_DIM = D_B // B_HEADS
B_CONV = 4
LRU_C = 8.0
D_C = D_MODEL
C_CONV = 3
D_FF = 4 * D_MODEL
EPS = 1e-6

SUBLANES = 8
V7X_VMEM_BYTES = 64 * 1024 * 1024
VMEM_LIMIT_BYTES = V7X_VMEM_BYTES - 8 * 1024 * 1024
ROW_TILE = 512
PROMPT_TILES_PER_STEP = 1
SAMPLE_TILES_PER_STEP = 1

F32 = jnp.float32
BF16 = jnp.bfloat16


class TileCfg(NamedTuple):
    n_seq: int
    seq_rows: int
    steps_per_seq: int
    gate_chunk: int
    emit_v: bool
    n_steps: int
    skewed: bool
    tiles: int

    @property
    def step_rows(self):
        return self.tiles * ROW_TILE

    @property
    def state_rows(self):
        groups = self.tiles if self.n_seq > 1 else 1
        return groups * self.n_seq * SUBLANES

    def tile_state(self, t):
        if self.n_seq == 1:
            return slice(0, SUBLANES)
        return slice(t * self.n_seq * SUBLANES, (t + 1) * self.n_seq * SUBLANES)


def _rmsnorm(x, g):
    ms = jnp.mean(x * x, axis=-1, keepdims=True)
    return x * lax.rsqrt(ms + EPS) * g


def _shift_rows(x, hist, k, cfg):
    if k == 0:
        return x
    rolled = pltpu.roll(x, k, 0)
    row = lax.broadcasted_iota(jnp.int32, (SUBLANES, x.shape[1]), 0)
    pieces = []
    for s in range(cfg.n_seq):
        lo = s * cfg.seq_rows
        h = pltpu.roll(hist[s * SUBLANES:(s + 1) * SUBLANES], k, 0)
        pieces.append(jnp.where(row < k, h, rolled[lo:lo + SUBLANES]))
        pieces.append(rolled[lo + SUBLANES:lo + cfg.seq_rows])
    return jnp.concatenate(pieces, axis=0)


def _causal_conv(x, hist, w, cfg):
    width = w.shape[0]
    y = w[0:1] * _shift_rows(x, hist, width - 1, cfg)
    for k in range(1, width):
        y = y + w[k:k + 1] * _shift_rows(x, hist, width - 1 - k, cfg)
    return y


def _seq_tails(x, cfg):
    pieces = [x[(s + 1) * cfg.seq_rows - SUBLANES:(s + 1) * cfg.seq_rows] for s in range(cfg.n_seq)]
    return pieces[0] if len(pieces) == 1 else jnp.concatenate(pieces, axis=0)


def _linear_scan(a, b, h_prev, cfg):
    row = lax.broadcasted_iota(jnp.int32, a.shape, 0) & (SUBLANES - 1)
    d = 1
    while d < SUBLANES:
        keep = row >= d
        a_sh = pltpu.roll(a, d, 0)
        b_sh = pltpu.roll(b, d, 0)
        b = jnp.where(keep, a * b_sh + b, b)
        a = jnp.where(keep, a * a_sh, a)
        d *= 2
    groups_per_seq = cfg.seq_rows // SUBLANES
    out = []
    for s in range(cfg.n_seq):
        carry = h_prev[s * SUBLANES + SUBLANES - 1:(s + 1) * SUBLANES]
        for g in range(groups_per_seq):
            lo = s * cfg.seq_rows + g * SUBLANES
            hg = b[lo:lo + SUBLANES] + a[lo:lo + SUBLANES] * carry
            out.append(hg)
            carry = hg[SUBLANES - 1:SUBLANES]
    return jnp.concatenate(out, axis=0)


def _spatial_gating(v, ws_ref, bias_ref, cfg):
    lc = cfg.gate_chunk
    n_chunks = v.shape[0] // lc
    vb = v.astype(BF16)
    pi = lax.shift_right_logical(lax.broadcasted_iota(jnp.int32, (lc, lc), 0), 6)
    pj = lax.shift_right_logical(lax.broadcasted_iota(jnp.int32, (lc, lc), 1), 6)
    mask = pj <= pi
    heads = []
    for h in range(A_HEADS):
        lanes = slice(h * A_HEAD_DIM, (h + 1) * A_HEAD_DIM)
        w = jnp.where(mask, ws_ref[h, 0:lc, 0:lc], 0.0).astype(BF16)
        vh = jnp.concatenate([vb[c * lc:(c + 1) * lc, lanes] for c in range(n_chunks)], axis=1)
        sh = jnp.dot(w, vh, preferred_element_type=F32)
        bias = bias_ref[h, 0:lc, :]
        heads.append(jnp.concatenate(
            [sh[:, c * A_HEAD_DIM:(c + 1) * A_HEAD_DIM] + bias for c in range(n_chunks)], axis=0))
    return jnp.concatenate(heads, axis=1)


def _mixer_in(x, gpre_ref, win_ref):
    hn = _rmsnorm(x, gpre_ref[...]).astype(BF16)
    return jnp.dot(hn, win_ref[...], preferred_element_type=F32)


def _mixer_out(x, mix, wout_ref, gpost_ref):
    m = jnp.dot(mix, wout_ref[...], preferred_element_type=F32)
    return x + _rmsnorm(m, gpost_ref[...])


def _mlp_up(x, gpre_ref, up_ref):
    hn = _rmsnorm(x, gpre_ref[...]).astype(BF16)
    return jnp.dot(hn, up_ref[...], preferred_element_type=F32)


def _mlp_down(x, hid, down_ref, gpost_ref):
    act = jnp.square(jnp.maximum(hid, 0.0)).astype(BF16)
    f = jnp.dot(act, down_ref[...], preferred_element_type=F32)
    return x + _rmsnorm(f, gpost_ref[...])


def _keep_state(cfg, new, old):
    return jnp.where(pl.program_id(0) < cfg.n_steps, new, old) if cfg.skewed else new


def _even_mixer(cfg, x, z, refs, states, tile, between):
    (vgain_ref, ws_ref, bias_ref, convw_ref, convb_ref, wgate_ref, ba_ref, bx_ref, lam_ref, wout_ref,
     gpost_ref) = refs
    hist_scr, h_scr, tailx_ref, tailh_ref, v_ref = states
    srows = cfg.tile_state(tile)
    u = jax.nn.gelu(z[:, 0:D_A])
    v = _rmsnorm(jax.nn.gelu(z[:, D_A:2 * D_A]), vgain_ref[...])
    xb = z[:, 2 * D_A:2 * D_A + D_B]
    gb = z[:, 2 * D_A + D_B:]
    if v_ref is not None:
        v_ref[tile * ROW_TILE:(tile + 1) * ROW_TILE, :] = v
    hist = hist_scr[srows, :]
    xc = _causal_conv(xb, hist, convw_ref[...], cfg) + convb_ref[...]
    new_hist = _keep_state(cfg, _seq_tails(xb, cfg), hist)
    hist_scr[srows, :] = new_hist
    tailx_ref[srows, :] = new_hist

    gate = _spatial_gating(v, ws_ref, bias_ref, cfg)
    half = D_B // 2
    xcb = xc.astype(BF16)
    g0 = jnp.dot(xcb[:, :half], wgate_ref[0], preferred_element_type=F32)
    g1 = jnp.dot(xcb[:, half:], wgate_ref[1], preferred_element_type=F32)
    between()

    a_out = u * gate
    r = jax.nn.sigmoid(jnp.concatenate([g0[:, :half], g1[:, :half]], axis=1) + ba_ref[...])
    ig = jax.nn.sigmoid(jnp.concatenate([g0[:, half:], g1[:, half:]], axis=1) + bx_ref[...])
    nl = -lam_ref[...]
    softplus = jnp.maximum(nl, 0.0) + jnp.log1p(jnp.exp(-jnp.abs(nl)))
    log_a = -LRU_C * r * softplus
    a = jnp.exp(log_a)
    bterm = jnp.sqrt(-jnp.tanh(log_a) * (a * a + 1.0)) * ig * xc
    h_prev = h_scr[srows, :]
    hseq = _linear_scan(a, bterm, h_prev, cfg)
    new_h = _keep_state(cfg, _seq_tails(hseq, cfg), h_prev)
    h_scr[srows, :] = new_h
    tailh_ref[srows, :] = new_h

    b_out = hseq * jax.nn.gelu(gb)
    mix = jnp.concatenate([a_out, b_out], axis=1).astype(BF16)
    return _mixer_out(x, mix, wout_ref, gpost_ref)


def _odd_mixer(cfg, x, z, refs, states, tile, between):
    convw_ref, wout_ref, gpost_ref = refs
    hist_scr, tail_ref = states
    srows = cfg.tile_state(tile)
    between()
    bg = z[:, 0:D_C]
    p = z[:, D_C:2 * D_C] * z[:, 2 * D_C:]
    hist = hist_scr[srows, :]
    y = _causal_conv(p, hist, convw_ref[...], cfg)
    new_hist = _keep_state(cfg, _seq_tails(p, cfg), hist)
    hist_scr[srows, :] = new_hist
    tail_ref[srows, :] = new_hist
    return _mixer_out(x, (bg * y).astype(BF16), wout_ref, gpost_ref)


def _paired(mixer, x, z_of, mlp_refs, mlp_x, write_mlp, up_first):
    gfpre_ref, up_ref, down_ref, gfpost_ref = mlp_refs
    if x is not None and mlp_x is not None:
        hn = _rmsnorm(mlp_x, gfpre_ref[...]).astype(BF16)
        hid = jnp.dot(hn, up_ref[:, 0:up_first], preferred_element_type=F32)
        z = z_of(x)
        if up_first < D_FF:
            rest = jnp.dot(hn, up_ref[:, up_first:], preferred_element_type=F32)
            hid = jnp.concatenate([hid, rest], axis=1)
    else:
        z = z_of(x) if x is not None else None
        hid = _mlp_up(mlp_x, gfpre_ref, up_ref) if mlp_x is not None else None

    def mlp_down():
        if mlp_x is not None:
            write_mlp(_mlp_down(mlp_x, hid, down_ref, gfpost_ref))

    if x is None:
        mlp_down()
        return None
    return mixer(x, z, mlp_down)


def _layer_body(cfg, mixer, z_of, mlp_refs, x_ref, xo_ref, x1_scr, up_first, step_variants):
    rows = lambda t: slice(t * ROW_TILE, (t + 1) * ROW_TILE)

    def write_out(t):
        def write(val):
            xo_ref[rows(t), :] = val
        return write

    if not cfg.skewed:
        x1_prev = None
        for t in range(cfg.tiles + 1):
            x = x_ref[rows(t), :] if t < cfg.tiles else None
            x1 = _paired(functools.partial(mixer, t), x, z_of, mlp_refs, x1_prev,
                         write_out(t - 1) if t > 0 else None, up_first)
            x1_prev = x1
        return

    i = pl.program_id(0)

    if not step_variants:
        @pl.when(i == 0)
        def _():
            x1_scr[...] = jnp.zeros_like(x1_scr)

        for t in range(cfg.tiles):
            x1_scr[rows(t), :] = _paired(functools.partial(mixer, t), x_ref[rows(t), :], z_of, mlp_refs,
                                         x1_scr[rows(t), :], write_out(t), up_first)
        return

    def step(do_mixer, do_mlp):
        for t in range(cfg.tiles):
            x = x_ref[rows(t), :] if do_mixer else None
            mlp_x = x1_scr[rows(t), :] if do_mlp else None
            x1 = _paired(functools.partial(mixer, t), x, z_of, mlp_refs, mlp_x, write_out(t), up_first)
            if do_mixer:
                x1_scr[rows(t), :] = x1

    pl.when(i == 0)(lambda: step(True, False))
    pl.when(jnp.logical_and(i > 0, i < cfg.n_steps))(lambda: step(True, True))
    pl.when(i == cfg.n_steps)(lambda: step(False, True))


def _load_states_at_sequence_start(cfg, pairs):
    i = pl.program_id(0)

    @pl.when(jnp.logical_and(lax.rem(i, cfg.steps_per_seq) == 0, i < cfg.n_steps))
    def _():
        for scr, src in pairs:
            scr[...] = src[...]


def _cast_next_layer_weights(cast_in_refs, cast_out_refs):
    for src, dst in zip(cast_in_refs, cast_out_refs):
        dst[...] = src[...].astype(BF16)


BIG_NAMES = ("win", "wout", "up", "down")

ROW_GPRE, ROW_GPOST, ROW_GFPRE, ROW_GFPOST, ROW_CCONV = 0, 1, 2, 3, 4
N_VEC_ROWS = ROW_CCONV + C_CONV
ROW_VGAIN, ROW_CONVB, ROW_BA, ROW_BX, ROW_LAM, ROW_BCONV = 0, 1, 2, 3, 4, 5
N_EVEN_VEC_ROWS = ROW_BCONV + B_CONV


def _row(ref, r, n=1):
    return ref.at[r:r + n, :]


def _even_layer_kernel(cfg, n_cast, *refs):
    refs = list(refs)
    x_ref, vec_ref, evec_ref, wsb_ref, wgate_ref, win_ref, wout_ref, up_ref, down_ref, state_in_ref = refs[:10]
    cast_in_refs = refs[10:10 + n_cast]
    del refs[:10 + n_cast]
    xo_ref, tails_ref = refs[:2]
    del refs[:2]
    v_ref = refs.pop(0) if cfg.emit_v else None
    gpre_ref, gpost_ref = _row(vec_ref, ROW_GPRE), _row(vec_ref, ROW_GPOST)
    gfpre_ref, gfpost_ref = _row(vec_ref, ROW_GFPRE), _row(vec_ref, ROW_GFPOST)
    vgain_ref, convb_ref = _row(evec_ref, ROW_VGAIN), _row(evec_ref, ROW_CONVB)
    ba_ref, bx_ref, lam_ref = _row(evec_ref, ROW_BA), _row(evec_ref, ROW_BX), _row(evec_ref, ROW_LAM)
    convw_ref = _row(evec_ref, ROW_BCONV, B_CONV)
    ws_ref, bias_ref = wsb_ref.at[0], wsb_ref.at[1]
    hist_in_ref, h_in_ref = state_in_ref.at[:, 0:D_B], state_in_ref.at[:, D_B:2 * D_B]
    tailx_ref, tailh_ref = tails_ref.at[:, 0:D_B], tails_ref.at[:, D_B:2 * D_B]
    cast_out_refs = refs[:n_cast]
    del refs[:n_cast]
    hist_scr, h_scr = refs[:2]
    x1_scr = refs[2] if cfg.skewed else None

    mixer_refs = (vgain_ref, ws_ref, bias_ref, convw_ref, convb_ref, wgate_ref, ba_ref, bx_ref, lam_ref,
                  wout_ref, gpost_ref)
    states = (hist_scr, h_scr, tailx_ref, tailh_ref, v_ref)
    _load_states_at_sequence_start(cfg, [(hist_scr, hist_in_ref), (h_scr, h_in_ref)])

    def mixer(tile, x, z, between):
        return _even_mixer(cfg, x, z, mixer_refs, states, tile, between)

    _layer_body(cfg, mixer, lambda x: _mixer_in(x, gpre_ref, win_ref), (gfpre_ref, up_ref, down_ref, gfpost_ref),
                x_ref, xo_ref, x1_scr, up_first=D_FF // 2, step_variants=False)
    _cast_next_layer_weights(cast_in_refs, cast_out_refs)


def _odd_layer_kernel(cfg, n_cast, *refs):
    refs = list(refs)
    x_ref, vec_ref, win_ref, wout_ref, up_ref, down_ref, hist_in_ref = refs[:7]
    cast_in_refs = refs[7:7 + n_cast]
    del refs[:7 + n_cast]
    gpre_ref, gpost_ref = _row(vec_ref, ROW_GPRE), _row(vec_ref, ROW_GPOST)
    gfpre_ref, gfpost_ref = _row(vec_ref, ROW_GFPRE), _row(vec_ref, ROW_GFPOST)
    convw_ref = _row(vec_ref, ROW_CCONV, C_CONV)
    xo_ref, tail_ref = refs[:2]
    del refs[:2]
    cast_out_refs = refs[:n_cast]
    del refs[:n_cast]
    hist_scr = refs[0]
    x1_scr = refs[1] if cfg.skewed else None

    _load_states_at_sequence_start(cfg, [(hist_scr, hist_in_ref)])

    def mixer(tile, x, z, between):
        return _odd_mixer(cfg, x, z, (convw_ref, wout_ref, gpost_ref), (hist_scr, tail_ref), tile, between)

    _layer_body(cfg, mixer, lambda x: _mixer_in(x, gpre_ref, win_ref), (gfpre_ref, up_ref, down_ref, gfpost_ref),
                x_ref, xo_ref, x1_scr, up_first=D_FF, step_variants=True)
    _cast_next_layer_weights(cast_in_refs, cast_out_refs)


def _resident(arr, layer):
    if layer is None:
        zeros = (0,) * arr.ndim
        return pl.BlockSpec(arr.shape, lambda i: zeros, pipeline_mode=pl.Buffered(1))
    zeros = (0,) * (arr.ndim - 1)
    return pl.BlockSpec((None,) + arr.shape[1:], lambda i: (layer,) + zeros, pipeline_mode=pl.Buffered(1))


def _cast_specs(cfg, cast_src):
    in_specs, out_specs, out_shape = [], [], []
    for arr, k in cast_src:
        _, n_rows, n_cols = arr.shape
        chunk = n_rows // cfg.n_steps
        in_specs.append(pl.BlockSpec((None, chunk, n_cols), lambda i, _k=k: (_k, _mixer_step(cfg, i), 0)))
        out_specs.append(pl.BlockSpec((chunk, n_cols), lambda i: (_mixer_step(cfg, i), 0)))
        out_shape.append(jax.ShapeDtypeStruct((n_rows, n_cols), BF16))
    return in_specs, out_specs, out_shape


def _mixer_step(cfg, i):
    return jnp.minimum(i, cfg.n_steps - 1)


def _state_in_spec(cfg, width, k):
    return pl.BlockSpec((None, cfg.state_rows, width),
                        lambda i: (k, _mixer_step(cfg, i) // cfg.steps_per_seq, 0))


def _state_spec(cfg, width):
    return pl.BlockSpec((cfg.state_rows, width), lambda i: (_mixer_step(cfg, i) // cfg.steps_per_seq, 0))


def _row_in_spec(cfg, width):
    return pl.BlockSpec((cfg.step_rows, width), lambda i: (_mixer_step(cfg, i), 0))


def _row_out_spec(cfg, width):
    if cfg.skewed:
        return pl.BlockSpec((cfg.step_rows, width), lambda i: (jnp.maximum(i - 1, 0), 0))
    return pl.BlockSpec((cfg.step_rows, width), lambda i: (i, 0))


def _grid(cfg):
    return (cfg.n_steps + 1,) if cfg.skewed else (cfg.n_steps,)


def _skew_scratch(cfg):
    return [pltpu.VMEM((cfg.step_rows, D_MODEL), F32)] if cfg.skewed else []


def _compiler_params():
    return pltpu.CompilerParams(dimension_semantics=("arbitrary",), vmem_limit_bytes=VMEM_LIMIT_BYTES)


def _even_layer(cfg, x, state, w, big, layer, cast_src=()):
    rows = x.shape[0]
    e = layer // 2
    consts = [(w["vec"], layer), (w["evec"], e), (w["wsb"], e), (w["wgate"], e)] + [(big[n], None) for n in BIG_NAMES]
    cast_in, cast_out, cast_shape = _cast_specs(cfg, cast_src)
    in_specs = ([_row_in_spec(cfg, D_MODEL)] + [_resident(c, k) for c, k in consts]
                + [_state_in_spec(cfg, 2 * D_B, e)] + cast_in)
    out_shape = [jax.ShapeDtypeStruct((rows, D_MODEL), F32),
                 jax.ShapeDtypeStruct((state.shape[1], 2 * D_B), F32)]
    out_specs = [_row_out_spec(cfg, D_MODEL), _state_spec(cfg, 2 * D_B)]
    if cfg.emit_v:
        out_shape.append(jax.ShapeDtypeStruct((rows, D_A), F32))
        out_specs.append(_row_in_spec(cfg, D_A))
    scratch = ([pltpu.VMEM((cfg.state_rows, D_B), F32), pltpu.VMEM((cfg.state_rows, D_B), F32)]
               + _skew_scratch(cfg))
    return pl.pallas_call(
        functools.partial(_even_layer_kernel, cfg, len(cast_src)),
        grid=_grid(cfg),
        in_specs=in_specs,
        out_specs=out_specs + cast_out,
        out_shape=out_shape + cast_shape,
        scratch_shapes=scratch,
        compiler_params=_compiler_params(),
        name="even_layer",
    )(x, *[c for c, _ in consts], state, *[arr for arr, _ in cast_src])


def _odd_layer(cfg, x, hist, w, big, layer, cast_src=()):
    rows = x.shape[0]
    o = layer // 2
    consts = [(w["vec"], layer)] + [(big[n], None) for n in BIG_NAMES]
    cast_in, cast_out, cast_shape = _cast_specs(cfg, cast_src)
    in_specs = ([_row_in_spec(cfg, D_MODEL)] + [_resident(c, k) for c, k in consts]
                + [_state_in_spec(cfg, D_C, o)] + cast_in)
    out_shape = [jax.ShapeDtypeStruct((rows, D_MODEL), F32),
                 jax.ShapeDtypeStruct((hist.shape[1], D_C), F32)]
    out_specs = [_row_out_spec(cfg, D_MODEL), _state_spec(cfg, D_C)]
    scratch = [pltpu.VMEM((cfg.state_rows, D_C), F32)] + _skew_scratch(cfg)
    return pl.pallas_call(
        functools.partial(_odd_layer_kernel, cfg, len(cast_src)),
        grid=_grid(cfg),
        in_specs=in_specs,
        out_specs=out_specs + cast_out,
        out_shape=out_shape + cast_shape,
        scratch_shapes=scratch,
        compiler_params=_compiler_params(),
        name="odd_layer",
    )(x, *[c for c, _ in consts], hist, *[arr for arr, _ in cast_src])


def _block_diag_gates(wa, wx):
    heads_per_half = B_HEADS // 2
    eye = jnp.eye(heads_per_half, dtype=wa.dtype)

    def bd(w):
        return jnp.einsum("eqhij,hg->eqhigj", w, eye).reshape(w.shape[0], 2, D_B // 2, D_B // 2)

    split = lambda w: w.reshape(w.shape[0], 2, heads_per_half, B_HEAD_DIM, B_HEAD_DIM)
    return jnp.concatenate([bd(split(wa)), bd(split(wx))], axis=-1).astype(BF16)


def _pad_state(rows):
    layers, n, r, c = rows.shape
    return jnp.pad(rows, ((0, 0), (0, 0), (SUBLANES - r, 0), (0, 0))).reshape(layers, n * SUBLANES, c)


def _small_weights(p):
    depth = p["norm_mix_pre"].shape[0]
    gains = jnp.stack([p["norm_mix_pre"], p["norm_mix_post"], p["norm_ffn_pre"], p["norm_ffn_post"]], axis=1)
    cconv = jnp.zeros((depth, C_CONV, D_C), F32).at[1::2].set(p["c_conv_w"])
    row = lambda v: v[:, None, :]
    evec = jnp.concatenate([row(p["a_v_gain"]), row(p["b_conv_b"]), row(p["b_ba"]), row(p["b_bx"]),
                            row(p["b_lambda"]), p["b_conv_w"]], axis=1)
    a_b_s = p["a_b_s"]
    bias = jnp.broadcast_to(a_b_s[..., None], a_b_s.shape + (A_HEAD_DIM,))
    vec = jnp.concatenate([gains, cconv], axis=1)
    assert vec.shape[1] == N_VEC_ROWS and evec.shape[1] == N_EVEN_VEC_ROWS
    return dict(vec=vec, evec=evec, wsb=jnp.stack([p["a_w_s"], bias], axis=1),
                wgate=_block_diag_gates(p["b_wa"], p["b_wx"]))


def _big_f32(p, layer):
    k = layer // 2
    if layer % 2 == 0:
        return [(p["w_in_even"], k), (p["w_out_even"], k), (p["mlp_up"], layer), (p["mlp_down"], layer)]
    return [(p["c_w_in"], k), (p["c_w_out"], k), (p["mlp_up"], layer), (p["mlp_down"], layer)]


def _trunk(x, b_state, c_conv, small, big, cfg, p=None):
    tails_b, tails_h, tails_c, v_rows = [], [], [], []
    for l in range(DEPTH):
        cast_src = _big_f32(p, l + 1) if p is not None and l + 1 < DEPTH else ()
        if l % 2 == 0:
            outs = _even_layer(cfg, x, b_state, small, big[l], l, cast_src)
            n_fixed = 3 if cfg.emit_v else 2
            tails_b.append(outs[1][:, :D_B])
            tails_h.append(outs[1][:, D_B:])
            if cfg.emit_v:
                v_rows.append(outs[2])
        else:
            outs = _odd_layer(cfg, x, c_conv, small, big[l], l, cast_src)
            n_fixed = 2
            tails_c.append(outs[1])
        x = outs[0]
        if cast_src:
            big[l + 1] = dict(zip(BIG_NAMES, outs[n_fixed:]))
    return x, tails_b, tails_h, tails_c, v_rows


def kernel(x_prompt, x_sample, cache_b_conv, state_b_h, cache_c_conv, norm_mix_pre, norm_mix_post, norm_ffn_pre, norm_ffn_post, w_in_even, a_v_gain, a_w_s, a_b_s, b_conv_w, b_conv_b, b_wa, b_ba, b_wx, b_bx, b_lambda, w_out_even, c_w_in, c_conv_w, c_w_out, mlp_up, mlp_down):
    batch, seq, _ = x_prompt.shape
    dec_batch, dec_seq, _ = x_sample.shape
    n_even, n_odd = w_in_even.shape[0], c_w_in.shape[0]
    prompt_rows, sample_rows = PROMPT_TILES_PER_STEP * ROW_TILE, SAMPLE_TILES_PER_STEP * ROW_TILE
    assert seq % prompt_rows == 0 and ROW_TILE % A_CHUNK == 0
    assert dec_seq <= A_CHUNK and dec_seq % SUBLANES == 0 and ROW_TILE % dec_seq == 0
    assert (dec_batch * dec_seq) % sample_rows == 0
    assert D_MODEL % (batch * seq // prompt_rows * 2 * SUBLANES) == 0
    p = dict(norm_mix_pre=norm_mix_pre, norm_mix_post=norm_mix_post, norm_ffn_pre=norm_ffn_pre,
             norm_ffn_post=norm_ffn_post, w_in_even=w_in_even, a_v_gain=a_v_gain, a_w_s=a_w_s, a_b_s=a_b_s,
             b_conv_w=b_conv_w, b_conv_b=b_conv_b, b_wa=b_wa, b_wx=b_wx, b_ba=b_ba, b_bx=b_bx,
             b_lambda=b_lambda, w_out_even=w_out_even, c_w_in=c_w_in, c_conv_w=c_conv_w, c_w_out=c_w_out,
             mlp_up=mlp_up, mlp_down=mlp_down)
    small = _small_weights(p)
    big = {0: {name: arr[k].astype(BF16) for name, (arr, k) in zip(BIG_NAMES, _big_f32(p, 0))}}

    cfg_p = TileCfg(n_seq=1, seq_rows=ROW_TILE, steps_per_seq=seq // prompt_rows, gate_chunk=A_CHUNK,
                    emit_v=False, n_steps=batch * seq // prompt_rows, skewed=True, tiles=PROMPT_TILES_PER_STEP)
    zb = jnp.zeros((n_even, batch * SUBLANES, 2 * D_B), F32)
    zc = jnp.zeros((n_odd, batch * SUBLANES, D_C), F32)
    y_p, tb_p, th_p, tc_p, _ = _trunk(x_prompt.reshape(batch * seq, D_MODEL), zb, zc, small, big, cfg_p, p)

    cfg_s = TileCfg(n_seq=ROW_TILE // dec_seq, seq_rows=dec_seq, steps_per_seq=1, gate_chunk=dec_seq,
                    emit_v=True, n_steps=dec_batch * dec_seq // sample_rows, skewed=False,
                    tiles=SAMPLE_TILES_PER_STEP)
    sb = jnp.concatenate([_pad_state(cache_b_conv), _pad_state(state_b_h[:, :, None, :])], axis=-1)
    sc = _pad_state(cache_c_conv)
    y_s, tb_s, th_s, tc_s, v_s = _trunk(x_sample.reshape(dec_batch * dec_seq, D_MODEL), sb, sc, small, big, cfg_s)

    def tails(ts, n, keep):
        t = jnp.stack(ts).reshape(len(ts), n, SUBLANES, -1)
        return t[:, :, SUBLANES - keep:, :]

    return (y_p.reshape(batch, seq, D_MODEL),
            y_s.reshape(dec_batch, dec_seq, D_MODEL),
            jnp.stack(v_s).reshape(n_even, dec_batch, dec_seq, D_A),
            tails(tb_p, batch, B_CONV - 1),
            tails(th_p, batch, 1)[:, :, 0, :],
            tails(tc_p, batch, C_CONV - 1),
            tails(tb_s, dec_batch, B_CONV - 1),
            tails(th_s, dec_batch, 1)[:, :, 0, :],
            tails(tc_s, dec_batch, C_CONV - 1))
```
